```python
import jax, jax.numpy as jnp
from jax import lax
import numpy as np

D_MODEL = 4096
BATCH = 4
SEQ = 4096
DEPTH = 4
DEC_BATCH = 2
DEC_SEQ = 4096
PAST_LEN = 128

HEAD_DIM = 128
MIX_WIDTH = D_MODEL
GM_WIDTH = MIX_WIDTH // 2
GM_GROUPS = GM_WIDTH // HEAD_DIM
CHUNK = 128
MLA_HEADS = (MIX_WIDTH - GM_WIDTH) // HEAD_DIM
QK_NOPE = 128
QK_ROPE = 64
V_HEAD = 128
QK_HEAD = QK_NOPE + QK_ROPE
Q_LORA = 896
KV_LORA = 512
ROPE_THETA = 10000.0
MLA_OUT = MLA_HEADS * V_HEAD
N_IN = 2 * GM_WIDTH + Q_LORA + KV_LORA + QK_ROPE
SPLITS = (GM_WIDTH, 2 * GM_WIDTH, 2 * GM_WIDTH + Q_LORA, 2 * GM_WIDTH + Q_LORA + KV_LORA)
MEM_LEN = 256
XA_HEADS = 4
XA_HEAD_DIM = 128
XA_WIDTH = XA_HEADS * XA_HEAD_DIM
D_FF = ((-(-8 * D_MODEL // 3) + 255) // 256) * 256
Q_BLOCK = 128
EPS = 1e-6

kernel_name = 'hymba_gmlp_mla_memory_encoder'


def rmsnorm(x, g):
    x32 = x.astype(jnp.float32)
    y = x32 * lax.rsqrt(jnp.mean(x32 * x32, axis=-1, keepdims=True) + EPS)
    return (y * g.astype(jnp.float32)).astype(x.dtype)


def rope_tables(seq, dtype):
    inv = 1.0 / (ROPE_THETA ** (jnp.arange(0, QK_ROPE, 2, dtype=jnp.float32) / QK_ROPE))
    ang = jnp.arange(seq, dtype=jnp.float32)[:, None] * inv[None, :]
    return jnp.cos(ang).astype(dtype), jnp.sin(ang).astype(dtype)


def apply_rope(x, cos, sin):
    x1, x2 = jnp.split(x, 2, axis=-1)
    return jnp.concatenate([x1 * cos - x2 * sin, x2 * cos + x1 * sin], axis=-1)


def spatial_gating(u, v, g_norm, w_s, b_s):
    b, s, _ = v.shape
    vn = rmsnorm(v, g_norm).reshape(b, s // CHUNK, CHUNK, GM_GROUPS, HEAD_DIM)
    mixed = jnp.einsum('gpq,bcqgd->bcpgd', w_s, vn) + b_s.T[None, None, :, :, None]
    return u * mixed.reshape(b, s, GM_WIDTH)


def mla(c_q, c_kv, k_rope, q_norm, w_uq, kv_norm, w_ukv, cos, sin):
    b, s, _ = c_q.shape
    q = (rmsnorm(c_q, q_norm) @ w_uq).reshape(b, s, MLA_HEADS, QK_HEAD)
    q_nope = q[..., :QK_NOPE]
    q_pe = apply_rope(q[..., QK_NOPE:], cos[:, None, :], sin[:, None, :])
    kv = (rmsnorm(c_kv, kv_norm) @ w_ukv).reshape(b, s, MLA_HEADS, QK_NOPE + V_HEAD)
    k_nope, v = kv[..., :QK_NOPE], kv[..., QK_NOPE:]
    k_pe = apply_rope(k_rope, cos, sin)
    scale = QK_HEAD ** -0.5
    nblk = s // Q_BLOCK
    qn_b = q_nope.reshape(b, nblk, Q_BLOCK, MLA_HEADS, QK_NOPE).transpose(1, 0, 2, 3, 4)
    qp_b = q_pe.reshape(b, nblk, Q_BLOCK, MLA_HEADS, QK_ROPE).transpose(1, 0, 2, 3, 4)

    def attend(blk):
        qn, qp = blk
        sc = (jnp.einsum('bqhd,bkhd->bhqk', qn, k_nope)
              + jnp.einsum('bqhr,bkr->bhqk', qp, k_pe)).astype(jnp.float32) * scale
        p = jax.nn.softmax(sc, axis=-1).astype(v.dtype)
        return jnp.einsum('bhqk,bkhd->bqhd', p, v)

    o = lax.map(attend, (qn_b, qp_b))
    return o.transpose(1, 0, 2, 3, 4).reshape(b, s, MLA_OUT)


def memory_cross_attention(h, mem, g_mem, wq, wk, wv, wo):
    b, s, _ = h.shape
    m = rmsnorm(mem, g_mem)
    n_mem = mem.shape[1]
    q = (h @ wq).reshape(b, s, XA_HEADS, XA_HEAD_DIM)
    k = (m @ wk).reshape(b, n_mem, XA_HEADS, XA_HEAD_DIM)
    v = (m @ wv).reshape(b, n_mem, XA_HEADS, XA_HEAD_DIM)
    sc = jnp.einsum('bqhd,bkhd->bhqk', q, k).astype(jnp.float32) * (XA_HEAD_DIM ** -0.5)
    p = jax.nn.softmax(sc, axis=-1).astype(v.dtype)
    o = jnp.einsum('bhqk,bkhd->bqhd', p, v).reshape(b, s, XA_WIDTH)
    return o @ wo


def trunk(x, mem, norm_mix, w_in, sg_norm, sg_w, sg_b, mla_q_norm, mla_w_uq, mla_kv_norm,
          mla_w_ukv, out_norm_a, out_norm_b, w_out, norm_xa, norm_mem, xa_wq, xa_wk, xa_wv,
          xa_wo, norm_ffn, w_gate, w_up, w_down, norm_final):
    cos, sin = rope_tables(x.shape[1], x.dtype)
    for l in range(DEPTH):
        h = rmsnorm(x, norm_mix[l])
        z = h @ w_in[l]
        zg = jax.nn.gelu(z[..., :2 * GM_WIDTH])
        u, v = zg[..., :GM_WIDTH], zg[..., GM_WIDTH:]
        _, _, c_q, c_kv, k_rope = jnp.split(z, SPLITS, axis=-1)
        ya = spatial_gating(u, v, sg_norm[l], sg_w[l], sg_b[l])
        yb = mla(c_q, c_kv, k_rope, mla_q_norm[l], mla_w_uq[l], mla_kv_norm[l], mla_w_ukv[l], cos, sin)
        y = jnp.concatenate([rmsnorm(ya, out_norm_a[l]), rmsnorm(yb, out_norm_b[l])], axis=-1)
        x = x + y @ w_out[l]
        x = x + memory_cross_attention(rmsnorm(x, norm_xa[l]), mem, norm_mem[l],
                                       xa_wq[l], xa_wk[l], xa_wv[l], xa_wo[l])
        h = rmsnorm(x, norm_ffn[l])
        x = x + (jax.nn.silu(h @ w_gate[l]) * (h @ w_up[l])) @ w_down[l]
    return rmsnorm(x, norm_final)


def setup_inputs(seed: int = 0) -> dict:
    key = jax.random.key(seed)
    ks = iter(jax.random.split(key, 40))

    def nrm(shape, scale):
        return jax.random.normal(next(ks), shape, jnp.float32) * scale

    def gain(shape):
        return 1.0 + 0.05 * jax.random.normal(next(ks), shape, jnp.float32)

    L, D = DEPTH, D_MODEL
    return {
        'x_prompt': nrm((BATCH, SEQ, D), 1.0),
        'x_sample': nrm((DEC_BATCH, DEC_SEQ, D), 1.0),
        'mem_prompt': nrm((BATCH, MEM_LEN, D), 1.0),
        'mem_sample': nrm((DEC_BATCH, MEM_LEN, D), 1.0),
        'norm_mix': gain((L, D)),
        'w_in': nrm((L, D, N_IN), D ** -0.5),
        'sg_norm': gain((L, GM_WIDTH)),
        'sg_w': nrm((L, GM_GROUPS, CHUNK, CHUNK), CHUNK ** -0.5),
        'sg_b': nrm((L, GM_GROUPS, CHUNK), 0.02),
        'mla_q_norm': gain((L, Q_LORA)),
        'mla_w_uq': nrm((L, Q_LORA, MLA_HEADS * QK_HEAD), Q_LORA ** -0.5),
        'mla_kv_norm': gain((L, KV_LORA)),
        'mla_w_ukv': nrm((L, KV_LORA, MLA_HEADS * (QK_NOPE + V_HEAD)), KV_LORA ** -0.5),
        'out_norm_a': gain((L, GM_WIDTH)),
        'out_norm_b': gain((L, MLA_OUT)),
        'w_out': nrm((L, GM_WIDTH + MLA_OUT, D), (GM_WIDTH + MLA_OUT) ** -0.5),
        'norm_xa': gain((L, D)),
        'norm_mem': gain((L, D)),
        'xa_wq': nrm((L, D, XA_WIDTH), D ** -0.5),
        'xa_wk': nrm((L, D, XA_WIDTH), D ** -0.5),
        'xa_wv': nrm((L, D, XA_WIDTH), D ** -0.5),
        'xa_wo': nrm((L, XA_WIDTH, D), XA_WIDTH ** -0.5),
        'norm_ffn': gain((L, D)),
        'w_gate': nrm((L, D, D_FF), D ** -0.5),
        'w_up': nrm((L, D, D_FF), D ** -0.5),
        'w_down': nrm((L, D_FF, D), D_FF ** -0.5),
        'norm_final': gain((D,)),
    }


def reference(x_prompt, x_sample, mem_prompt, mem_sample, norm_mix, w_in, sg_norm, sg_w, sg_b,
              mla_q_norm, mla_w_uq, mla_kv_norm, mla_w_ukv, out_norm_a, out_norm_b, w_out,
              norm_xa, norm_mem, xa_wq, xa_wk, xa_wv, xa_wo, norm_ffn, w_gate, w_up, w_down,
              norm_final):
    weights = (norm_mix, w_in, sg_norm, sg_w, sg_b, mla_q_norm, mla_w_uq, mla_kv_norm,
               mla_w_ukv, out_norm_a, out_norm_b, w_out, norm_xa, norm_mem, xa_wq, xa_wk,
               xa_wv, xa_wo, norm_ffn, w_gate, w_up, w_down, norm_final)
    y_prompt = trunk(x_prompt, mem_prompt, *weights)
    y_sample = trunk(x_sample, mem_sample, *weights)
    return (y_prompt, y_sample)
```

```python
import functools
import math

import jax
import jax.numpy as jnp
from jax import lax
from jax.experimental import pallas as pl
from jax.experimental.pallas import tpu as pltpu

F32 = jnp.float32
BF16 = jnp.bfloat16

EPS = 1e-6
ROPE_THETA = 10000.0
HEAD_DIM = 128
QK_ROPE = 64
ROPE_HALF = QK_ROPE // 2
QK_PAD = 2 * HEAD_DIM
LANES = 128
FF_BLOCK = 512
V7X_VMEM_LIMIT_BYTES = 60000 * 1024

_NT_DIMS = (((1,), (1,)), ((), ()))


def _params(*semantics):
    return pltpu.CompilerParams(dimension_semantics=semantics,
                                vmem_limit_bytes=V7X_VMEM_LIMIT_BYTES)


def _tile(n, pref):
    t = min(n, pref)
    while n % t:
        t //= 2
    return t


def _rms(x, g):
    r = lax.rsqrt(jnp.mean(x * x, axis=-1, keepdims=True) + EPS)
    return (x * r) * g


def _norm_kernel(x_ref, g_ref, o_ref):
    o_ref[...] = _rms(x_ref[...], g_ref[...]).astype(o_ref.dtype)


def _rmsnorm(x, g, out_dtype, *, row_start=0, rows=None, name):
    d = x.shape[1]
    rows = x.shape[0] if rows is None else rows
    tm = _tile(math.gcd(rows, row_start) if row_start else rows, 512)
    off = row_start // tm
    return pl.pallas_call(
        _norm_kernel,
        out_shape=jax.ShapeDtypeStruct((rows, d), out_dtype),
        grid=(rows // tm,),
        in_specs=[pl.BlockSpec((tm, d), lambda i: (i + off, 0)),
                  pl.BlockSpec((1, d), lambda i: (0, 0))],
        out_specs=pl.BlockSpec((tm, d), lambda i: (i, 0)),
        compiler_params=_params("parallel"),
        name=name,
    )(x, g.reshape(1, d))


def _mm_kernel(x_ref, w_ref, o_ref, *, act):
    acc = jnp.dot(x_ref[...], w_ref[...], preferred_element_type=F32)
    if act == "gelu":
        acc = jax.nn.gelu(acc)
    elif act == "swiglu":
        half = acc.shape[1] // 2
        acc = jax.nn.silu(acc[:, :half]) * acc[:, half:]
    o_ref[...] = acc.astype(o_ref.dtype)


def _matmul(x, w, *, act, out_dtype, tm, tn, name):
    m, k = x.shape
    n = w.shape[1]
    tm, tn = _tile(m, tm), _tile(n, tn)
    out_div = 2 if act == "swiglu" else 1
    return pl.pallas_call(
        functools.partial(_mm_kernel, act=act),
        out_shape=jax.ShapeDtypeStruct((m, n // out_div), out_dtype),
        grid=(m // tm, n // tn),
        in_specs=[pl.BlockSpec((tm, k), lambda i, j: (i, 0)),
                  pl.BlockSpec((k, tn), lambda i, j: (0, j))],
        out_specs=pl.BlockSpec((tm, tn // out_div), lambda i, j: (i, j)),
        compiler_params=_params("parallel", "arbitrary"),
        name=name,
    )(x, w)


def _mix_out_kernel(ya_ref, yb_ref, wa_ref, wb_ref, x_ref, o_ref):
    acc = jnp.dot(ya_ref[...], wa_ref[...], preferred_element_type=F32)
    acc += jnp.dot(yb_ref[...], wb_ref[...], preferred_element_type=F32)
    o_ref[...] = x_ref[...] + acc


def _mix_out(ya, yb, w, x, *, name):
    m, ka = ya.shape
    kb = yb.shape[1]
    assert ka == kb
    n = w.shape[1]
    tm, tn = _tile(m, 1024), _tile(n, 512)
    return pl.pallas_call(
        _mix_out_kernel,
        out_shape=jax.ShapeDtypeStruct((m, n), F32),
        grid=(m // tm, n // tn),
        in_specs=[pl.BlockSpec((tm, ka), lambda i, j: (i, 0)),
                  pl.BlockSpec((tm, kb), lambda i, j: (i, 0)),
                  pl.BlockSpec((ka, tn), lambda i, j: (0, j)),
                  pl.BlockSpec((kb, tn), lambda i, j: (1, j)),
                  pl.BlockSpec((tm, tn), lambda i, j: (i, j))],
        out_specs=pl.BlockSpec((tm, tn), lambda i, j: (i, j)),
        compiler_params=_params("parallel", "arbitrary"),
        name=name,
    )(ya, yb, w, w, x)


def _down_kernel(a_ref, w_ref, x_ref, o_ref):
    part = jnp.dot(a_ref[...], w_ref[...], preferred_element_type=F32)

    @pl.when(pl.program_id(2) == 0)
    def _():
        o_ref[...] = x_ref[...] + part

    @pl.when(pl.program_id(2) != 0)
    def _():
        o_ref[...] += part


def _down_proj(a, w, x, *, name):
    m, k = a.shape
    n = w.shape[1]
    tm, tn = _tile(m, 1024), _tile(n, 1024)
    nk = 4 if (k % (4 * 2 * LANES) == 0 and k > 4096) else 1
    tk = k // nk
    return pl.pallas_call(
        _down_kernel,
        out_shape=jax.ShapeDtypeStruct((m, n), F32),
        grid=(m // tm, n // tn, nk),
        in_specs=[pl.BlockSpec((tm, tk), lambda i, j, kk: (i, kk)),
                  pl.BlockSpec((tk, tn), lambda i, j, kk: (kk, j)),
                  pl.BlockSpec((tm, tn), lambda i, j, kk: (i, j))],
        out_specs=pl.BlockSpec((tm, tn), lambda i, j, kk: (i, j)),
        compiler_params=_params("parallel", "arbitrary", "arbitrary"),
        name=name,
    )(a, w, x)


def _gate_kernel(u_ref, v_ref, gv_ref, ws_ref, b_ref, ga_ref, o_ref, mix_ref, *, n_chunks, chunk):
    groups = ws_ref.shape[0]
    vn = _rms(v_ref[...].astype(F32), gv_ref[...]).astype(BF16)
    for c in range(n_chunks):
        rows = slice(c * chunk, (c + 1) * chunk)
        for g in range(groups):
            cols = slice(g * HEAD_DIM, (g + 1) * HEAD_DIM)
            mix_ref[rows, cols] = jnp.dot(ws_ref[g], vn[rows, cols], preferred_element_type=F32)
        mix_ref[rows, :] += b_ref[...]
    ya = u_ref[...].astype(F32) * mix_ref[...]
    o_ref[...] = _rms(ya, ga_ref[...]).astype(o_ref.dtype)


def _spatial_gate(zg, g_v, w_s, b_full, g_a, *, name):
    t, two_gw = zg.shape
    gw = two_gw // 2
    chunk = w_s.shape[1]
    tm = _tile(t, 2 * chunk)
    return pl.pallas_call(
        functools.partial(_gate_kernel, n_chunks=tm // chunk, chunk=chunk),
        out_shape=jax.ShapeDtypeStruct((t, gw), BF16),
        grid=(t // tm,),
        in_specs=[pl.BlockSpec((tm, gw), lambda i: (i, 0)),
                  pl.BlockSpec((tm, gw), lambda i: (i, 1)),
                  pl.BlockSpec((1, gw), lambda i: (0, 0)),
                  pl.BlockSpec(w_s.shape, lambda i: (0, 0, 0)),
                  pl.BlockSpec((chunk, gw), lambda i: (0, 0)),
                  pl.BlockSpec((1, gw), lambda i: (0, 0))],
        out_specs=pl.BlockSpec((tm, gw), lambda i: (i, 0)),
        scratch_shapes=[pltpu.VMEM((tm, gw), F32)],
        compiler_params=_params("parallel"),
        name=name,
    )(zg, zg, g_v.reshape(1, gw), w_s, b_full, g_a.reshape(1, gw))


def _rope(slab, cos_t, sin_t):
    return slab * cos_t + pltpu.roll(slab, QK_ROPE, 1) * sin_t


def _mla_proj_kernel(zc_ref, gq_ref, gkv_ref, wq_ref, wkv_ref, cos_ref, sin_ref,
                     q_ref, k_ref, v_ref, hq_s, hkv_s, kpe_s, *, kv_lora, q_scale):
    heads = q_ref.shape[0]
    cos_t, sin_t = cos_ref[...], sin_ref[...]

    @pl.when(pl.program_id(1) == 0)
    def _():
        zc = zc_ref[...]
        hkv_s[...] = _rms(zc[:, :kv_lora], gkv_ref[...]).astype(BF16)
        kpe_s[...] = _rope(zc[:, kv_lora:kv_lora + LANES], cos_t, sin_t).astype(BF16)
        hq_s[...] = _rms(zc[:, kv_lora + LANES:], gq_ref[...]).astype(BF16)

    q_raw = jnp.dot(hq_s[...], wq_ref[...], preferred_element_type=F32)
    kv_raw = jnp.dot(hkv_s[...], wkv_ref[...], preferred_element_type=F32)
    for h in range(heads):
        base = h * QK_PAD
        q_ref[h, :, :HEAD_DIM] = (q_raw[:, base:base + HEAD_DIM] * q_scale).astype(BF16)
        q_pe = _rope(q_raw[:, base + HEAD_DIM:base + QK_PAD], cos_t, sin_t)
        q_ref[h, :, HEAD_DIM:] = (q_pe * q_scale).astype(BF16)
        k_ref[h, :, :HEAD_DIM] = kv_raw[:, base:base + HEAD_DIM].astype(BF16)
        k_ref[h, :, HEAD_DIM:] = kpe_s[...]
        v_ref[h] = kv_raw[:, base + HEAD_DIM:base + QK_PAD].astype(BF16)


def _mla_proj(zc, g_q, g_kv, w_q, w_kv, cos_t, sin_t, *, seq, q_scale, name):
    t, zw = zc.shape
    kv_lora = g_kv.shape[0]
    q_lora = g_q.shape[0]
    heads = w_q.shape[1] // QK_PAD
    hb = _tile(heads, 4)
    tm = _tile(seq, 1024)
    n_pos = seq // tm
    return pl.pallas_call(
        functools.partial(_mla_proj_kernel, kv_lora=kv_lora, q_scale=q_scale),
        out_shape=(jax.ShapeDtypeStruct((heads, t, QK_PAD), BF16),
                   jax.ShapeDtypeStruct((heads, t, QK_PAD), BF16),
                   jax.ShapeDtypeStruct((heads, t, HEAD_DIM), BF16)),
        grid=(t // tm, heads // hb),
        in_specs=[pl.BlockSpec((tm, zw), lambda i, h: (i, 0)),
                  pl.BlockSpec((1, q_lora), lambda i, h: (0, 0)),
                  pl.BlockSpec((1, kv_lora), lambda i, h: (0, 0)),
                  pl.BlockSpec((q_lora, hb * QK_PAD), lambda i, h: (0, h)),
                  pl.BlockSpec((kv_lora, hb * QK_PAD), lambda i, h: (0, h)),
                  pl.BlockSpec((tm, LANES), lambda i, h: (i % n_pos, 0)),
                  pl.BlockSpec((tm, LANES), lambda i, h: (i % n_pos, 0))],
        out_specs=(pl.BlockSpec((hb, tm, QK_PAD), lambda i, h: (h, i, 0)),
                   pl.BlockSpec((hb, tm, QK_PAD), lambda i, h: (h, i, 0)),
                   pl.BlockSpec((hb, tm, HEAD_DIM), lambda i, h: (h, i, 0))),
        scratch_shapes=[pltpu.VMEM((tm, q_lora), BF16),
                        pltpu.VMEM((tm, kv_lora), BF16),
                        pltpu.VMEM((tm, LANES), BF16)],
        compiler_params=_params("parallel", "arbitrary"),
        name=name,
    )(zc, g_q.reshape(1, q_lora), g_kv.reshape(1, kv_lora), w_q, w_kv, cos_t, sin_t)


def _attn_kernel(q_ref, k_ref, v_ref, g_ref, o_ref, acc_s, *, tk):
    heads = acc_s.shape[0]
    h = pl.program_id(2)
    q = q_ref[0]
    tq = q.shape[0]
    n_kv = k_ref.shape[1] // tk

    def body(c, carry):
        m, l, acc = carry
        start = pl.multiple_of(c * tk, tk)
        s = lax.dot_general(q, k_ref[0, pl.ds(start, tk), :], _NT_DIMS, preferred_element_type=F32)
        m_new = jnp.maximum(m, jnp.max(s, axis=-1, keepdims=True))
        alpha = jnp.exp(m - m_new)
        p = jnp.exp(s - m_new)
        l = alpha * l + jnp.sum(p, axis=-1, keepdims=True)
        pv = jnp.dot(p.astype(BF16), v_ref[0, pl.ds(start, tk), :], preferred_element_type=F32)
        return m_new, l, alpha * acc + pv

    init = (jnp.full((tq, 1), -jnp.inf, F32), jnp.zeros((tq, 1), F32), jnp.zeros((tq, HEAD_DIM), F32))
    _, l, acc = lax.fori_loop(0, n_kv, body, init)
    acc_s[h] = acc * (1.0 / l)

    @pl.when(h == heads - 1)
    def _():
        ssq = jnp.zeros((tq, 1), F32)
        for hh in range(heads):
            o = acc_s[hh]
            ssq += jnp.sum(o * o, axis=-1, keepdims=True)
        r = lax.rsqrt(ssq * (1.0 / (heads * HEAD_DIM)) + EPS)
        for hh in range(heads):
            cols = slice(hh * HEAD_DIM, (hh + 1) * HEAD_DIM)
            o_ref[:, cols] = ((acc_s[hh] * r) * g_ref[:, cols]).astype(o_ref.dtype)


def _mla_attention(q, k, v, g_out, *, seq, name):
    heads, t, _ = q.shape
    batch = t // seq
    tq = _tile(seq, 512)
    tk = _tile(seq, 512)
    nq = seq // tq
    width = heads * HEAD_DIM
    return pl.pallas_call(
        functools.partial(_attn_kernel, tk=tk),
        out_shape=jax.ShapeDtypeStruct((t, width), BF16),
        grid=(batch, nq, heads),
        in_specs=[pl.BlockSpec((1, tq, QK_PAD), lambda b, i, h: (h, b * nq + i, 0)),
                  pl.BlockSpec((1, seq, QK_PAD), lambda b, i, h: (h, b, 0)),
                  pl.BlockSpec((1, seq, HEAD_DIM), lambda b, i, h: (h, b, 0)),
                  pl.BlockSpec((1, width), lambda b, i, h: (0, 0))],
        out_specs=pl.BlockSpec((tq, width), lambda b, i, h: (b * nq + i, 0)),
        scratch_shapes=[pltpu.VMEM((heads, tq, HEAD_DIM), F32)],
        compiler_params=_params("parallel", "parallel", "arbitrary"),
        name=name,
    )(q, k, v, g_out.reshape(1, width))


def _xa_kv_kernel(mem_ref, g_ref, w_ref, k_ref, v_ref):
    m = _rms(mem_ref[...], g_ref[...]).astype(BF16)
    kv = jnp.dot(m, w_ref[...], preferred_element_type=F32)
    half = kv.shape[1] // 2
    k_ref[...] = kv[:, :half].astype(BF16)
    v_ref[...] = kv[:, half:].astype(BF16)


def _xa_kv(mem, g_mem, w_kv, *, name):
    rows, d = mem.shape
    xw = w_kv.shape[1] // 2
    tm = _tile(rows, 256)
    return pl.pallas_call(
        _xa_kv_kernel,
        out_shape=(jax.ShapeDtypeStruct((rows, xw), BF16), jax.ShapeDtypeStruct((rows, xw), BF16)),
        grid=(rows // tm,),
        in_specs=[pl.BlockSpec((tm, d), lambda i: (i, 0)),
                  pl.BlockSpec((1, d), lambda i: (0, 0)),
                  pl.BlockSpec((d, 2 * xw), lambda i: (0, 0))],
        out_specs=(pl.BlockSpec((tm, xw), lambda i: (i, 0)), pl.BlockSpec((tm, xw), lambda i: (i, 0))),
        compiler_params=_params("parallel"),
        name=name,
    )(mem, g_mem.reshape(1, d), w_kv)


def _xa_kernel(x_ref, gx_ref, wq_ref, k_ref, v_ref, wo_ref, gf_ref, x2_ref, h_ref, o_s, *, scale):
    x = x_ref[...]
    hx = _rms(x, gx_ref[...]).astype(BF16)
    q = (jnp.dot(hx, wq_ref[...], preferred_element_type=F32) * scale).astype(BF16)
    k = k_ref[0]
    v = v_ref[0]
    for hh in range(q.shape[1] // HEAD_DIM):
        cols = slice(hh * HEAD_DIM, (hh + 1) * HEAD_DIM)
        s = lax.dot_general(q[:, cols], k[:, cols], _NT_DIMS, preferred_element_type=F32)
        e = jnp.exp(s - jnp.max(s, axis=-1, keepdims=True))
        p = e * (1.0 / jnp.sum(e, axis=-1, keepdims=True))
        o_s[:, cols] = jnp.dot(p.astype(BF16), v[:, cols], preferred_element_type=F32).astype(BF16)
    x2 = x + jnp.dot(o_s[...], wo_ref[...], preferred_element_type=F32)
    x2_ref[...] = x2
    h_ref[...] = _rms(x2, gf_ref[...]).astype(h_ref.dtype)


def _cross_attention(x, g_xa, w_q, k_mem, v_mem, w_o, g_ffn, *, seq, name):
    t, d = x.shape
    _, n_mem, xw = k_mem.shape
    tm = _tile(seq, 256)
    per_seq = seq // tm
    return pl.pallas_call(
        functools.partial(_xa_kernel, scale=HEAD_DIM ** -0.5),
        out_shape=(jax.ShapeDtypeStruct((t, d), F32), jax.ShapeDtypeStruct((t, d), BF16)),
        grid=(t // tm,),
        in_specs=[pl.BlockSpec((tm, d), lambda i: (i, 0)),
                  pl.BlockSpec((1, d), lambda i: (0, 0)),
                  pl.BlockSpec((d, xw), lambda i: (0, 0)),
                  pl.BlockSpec((1, n_mem, xw), lambda i: (i // per_seq, 0, 0)),
                  pl.BlockSpec((1, n_mem, xw), lambda i: (i // per_seq, 0, 0)),
                  pl.BlockSpec((xw, d), lambda i: (0, 0)),
                  pl.BlockSpec((1, d), lambda i: (0, 0))],
        out_specs=(pl.BlockSpec((tm, d), lambda i: (i, 0)), pl.BlockSpec((tm, d), lambda i: (i, 0))),
        scratch_shapes=[pltpu.VMEM((tm, xw), BF16)],
        compiler_params=_params("parallel"),
        name=name,
    )(x, g_xa.reshape(1, d), w_q, k_mem, v_mem, w_o, g_ffn.reshape(1, d))


def _rope_tables(seq):
    inv = 1.0 / (ROPE_THETA ** (jnp.arange(0, QK_ROPE, 2, dtype=F32) / QK_ROPE))
    ang = jnp.arange(seq, dtype=F32)[:, None] * inv[None, :]
    cos, sin = jnp.cos(ang), jnp.sin(ang)
    zeros = jnp.zeros((seq, LANES - QK_ROPE), F32)
    return (jnp.concatenate([cos, cos, zeros], axis=1),
            jnp.concatenate([-sin, sin, zeros], axis=1))


def _swap_halves(w):
    return jnp.concatenate([w[..., ROPE_HALF:], w[..., :ROPE_HALF]], axis=-1)


def _layout_w_in(w_in, gw, q_lora, kv_lora):
    w_c_q = w_in[:, 2 * gw:2 * gw + q_lora]
    w_c_kv = w_in[:, 2 * gw + q_lora:2 * gw + q_lora + kv_lora]
    w_kr = w_in[:, 2 * gw + q_lora + kv_lora:]
    latent = jnp.concatenate([w_c_kv, w_kr, _swap_halves(w_kr), w_c_q], axis=1)
    return w_in[:, :2 * gw].astype(BF16), latent.astype(BF16)


def _layout_w_uq(w_uq, heads):
    q_lora = w_uq.shape[0]
    w = w_uq.reshape(q_lora, heads, HEAD_DIM + QK_ROPE)
    rope = w[..., HEAD_DIM:]
    w = jnp.concatenate([w[..., :HEAD_DIM], rope, _swap_halves(rope)], axis=-1)
    return w.reshape(q_lora, heads * QK_PAD).astype(BF16)


def _layout_ffn(w_gate, w_up, w_down):
    d, d_ff = w_gate.shape
    ff_pad = -(-d_ff // FF_BLOCK) * FF_BLOCK
    pad = ((0, 0), (0, ff_pad - d_ff))
    g = jnp.pad(w_gate, pad).reshape(d, ff_pad // FF_BLOCK, 1, FF_BLOCK)
    u = jnp.pad(w_up, pad).reshape(d, ff_pad // FF_BLOCK, 1, FF_BLOCK)
    w_gu = jnp.concatenate([g, u], axis=2).reshape(d, 2 * ff_pad).astype(BF16)
    w_dn = jnp.pad(w_down, ((0, ff_pad - d_ff), (0, 0))).astype(BF16)
    return w_gu, w_dn


def kernel(x_prompt, x_sample, mem_prompt, mem_sample, norm_mix, w_in, sg_norm, sg_w, sg_b, mla_q_norm, mla_w_uq, mla_kv_norm, mla_w_ukv, out_norm_a, out_norm_b, w_out, norm_xa, norm_mem, xa_wq, xa_wk, xa_wv, xa_wo, norm_ffn, w_gate, w_up, w_down, norm_final):
    depth, d = norm_mix.shape
    seq = x_prompt.shape[1]
    assert x_sample.shape[1] == seq and mem_sample.shape[1] == mem_prompt.shape[1]
    gw = sg_norm.shape[1]
    chunk = sg_w.shape[2]
    q_lora, kv_lora = mla_q_norm.shape[1], mla_kv_norm.shape[1]
    heads = out_norm_b.shape[1] // HEAD_DIM
    n_mem = mem_prompt.shape[1]
    assert w_in.shape[2] == 2 * gw + q_lora + kv_lora + QK_ROPE
    assert mla_w_uq.shape[2] == heads * (HEAD_DIM + QK_ROPE) and mla_w_ukv.shape[2] == heads * QK_PAD

    rows_p = x_prompt.shape[0] * seq
    rows_s = x_sample.shape[0] * seq
    x = jnp.concatenate([x_prompt.reshape(rows_p, d), x_sample.reshape(rows_s, d)], axis=0)
    mem = jnp.concatenate([mem_prompt.reshape(-1, d), mem_sample.reshape(-1, d)], axis=0)
    batch = (rows_p + rows_s) // seq

    cos_t, sin_t = _rope_tables(seq)
    q_scale = (HEAD_DIM + QK_ROPE) ** -0.5

    h_mix = _rmsnorm(x, norm_mix[0], BF16, name="norm_mix0")
    for l in range(depth):
        w_gated, w_latent = _layout_w_in(w_in[l], gw, q_lora, kv_lora)
        zg = _matmul(h_mix, w_gated, act="gelu", out_dtype=BF16, tm=1024, tn=1024, name=f"w_in_gated{l}")
        zc = _matmul(h_mix, w_latent, act=None, out_dtype=F32, tm=1024, tn=512, name=f"w_in_latent{l}")

        b_full = jnp.repeat(sg_b[l].T, HEAD_DIM, axis=1)
        ya = _spatial_gate(zg, sg_norm[l], sg_w[l].astype(BF16), b_full, out_norm_a[l], name=f"gate{l}")

        q, k, v = _mla_proj(zc, mla_q_norm[l], mla_kv_norm[l], _layout_w_uq(mla_w_uq[l], heads),
                            mla_w_ukv[l].astype(BF16), cos_t, sin_t, seq=seq, q_scale=q_scale,
                            name=f"mla_proj{l}")
        yb = _mla_attention(q, k, v, out_norm_b[l], seq=seq, name=f"mla_attn{l}")

        x = _mix_out(ya, yb, w_out[l].astype(BF16), x, name=f"w_out{l}")

        w_xkv = jnp.concatenate([xa_wk[l], xa_wv[l]], axis=1).astype(BF16)
        k_mem, v_mem = _xa_kv(mem, norm_mem[l], w_xkv, name=f"xa_kv{l}")
        xw = k_mem.shape[1]
        x, h_ffn = _cross_attention(x, norm_xa[l], xa_wq[l].astype(BF16),
                                    k_mem.reshape(batch, n_mem, xw), v_mem.reshape(batch, n_mem, xw),
                                    xa_wo[l].astype(BF16), norm_ffn[l], seq=seq, name=f"xattn{l}")

        w_gu, w_dn = _layout_ffn(w_gate[l], w_up[l], w_down[l])
        a = _matmul(h_ffn, w_gu, act="swiglu", out_dtype=BF16, tm=1024, tn=2 * FF_BLOCK, name=f"ffn_up{l}")
        x = _down_proj(a, w_dn, x, name=f"ffn_down{l}")
        if l + 1 < depth:
            h_mix = _rmsnorm(x, norm_mix[l + 1], BF16, name=f"norm_mix{l + 1}")

    y_prompt = _rmsnorm(x, norm_final, F32, row_start=0, rows=rows_p, name="norm_final_prompt")
    y_sample = _rmsnorm(x, norm_final, F32, row_start=rows_p, rows=rows_s, name="norm_final_sample")
    return (y_prompt.reshape(x_prompt.shape), y_sample.reshape(x_sample.shape))
```

```python
import functools
import math

import jax
import jax.numpy as jnp
from jax import lax
from jax.experimental import pallas as pl
from jax.experimental.pallas import tpu as pltpu

F32 = jnp.float32
BF16 = jnp.bfloat16

EPS = 1e-6
ROPE_THETA = 10000.0
HEAD_DIM = 128
QK_ROPE = 64
ROPE_HALF = QK_ROPE // 2
QK_PAD = 2 * HEAD_DIM
LANES = 128
FF_BLOCK = 512
V7X_VMEM_LIMIT_BYTES = 60000 * 1024

_NT_DIMS = (((1,), (1,)), ((), ()))


def _params(*semantics):
    return pltpu.CompilerParams(dimension_semantics=semantics,
                                vmem_limit_bytes=V7X_VMEM_LIMIT_BYTES)


def _tile(n, pref):
    t = min(n, pref)
    while n % t:
        t //= 2
    return t


def _rms(x, g):
    r = lax.rsqrt(jnp.mean(x * x, axis=-1, keepdims=True) + EPS)
    return (x * r) * g


def _norm_kernel(x_ref, g_ref, o_ref):
    o_ref[...] = _rms(x_ref[...], g_ref[...]).astype(o_ref.dtype)


def _rmsnorm(x, g, out_dtype, *, row_start=0, rows=None, name):
    d = x.shape[1]
    rows = x.shape[0] if rows is None else rows
    tm = _tile(math.gcd(rows, row_start) if row_start else rows, 512)
    off = row_start // tm
    return pl.pallas_call(
        _norm_kernel,
        out_shape=jax.ShapeDtypeStruct((rows, d), out_dtype),
        grid=(rows // tm,),
        in_specs=[pl.BlockSpec((tm, d), lambda i: (i + off, 0)),
                  pl.BlockSpec((1, d), lambda i: (0, 0))],
        out_specs=pl.BlockSpec((tm, d), lambda i: (i, 0)),
        compiler_params=_params("parallel"),
        name=name,
    )(x, g.reshape(1, d))


def _mm_kernel(x_ref, w_ref, o_ref, *, act):
    acc = jnp.dot(x_ref[...], w_ref[...], preferred_element_type=F32)
    if act == "gelu":
        acc = jax.nn.gelu(acc)
    elif act == "swiglu":
        half = acc.shape[1] // 2
        acc = jax.nn.silu(acc[:, :half]) * acc[:, half:]
    o_ref[...] = acc.astype(o_ref.dtype)


def _matmul(x, w, *, act, out_dtype, tm, tn, name, layer=None):
    m, k = x.shape
    n = w.shape[-1]
    tm, tn = _tile(m, tm), _tile(n, tn)
    out_div = 2 if act == "swiglu" else 1
    if layer is None:
        w_spec = pl.BlockSpec((k, tn), lambda i, j: (0, j))
    else:
        w_spec = pl.BlockSpec((None, k, tn), lambda i, j: (layer, 0, j))
    return pl.pallas_call(
        functools.partial(_mm_kernel, act=act),
        out_shape=jax.ShapeDtypeStruct((m, n // out_div), out_dtype),
        grid=(m // tm, n // tn),
        in_specs=[pl.BlockSpec((tm, k), lambda i, j: (i, 0)), w_spec],
        out_specs=pl.BlockSpec((tm, tn // out_div), lambda i, j: (i, j)),
        compiler_params=_params("parallel", "arbitrary"),
        name=name,
    )(x, w)


def _mix_out_kernel(ya_ref, yb_ref, wa_ref, wb_ref, x_ref, o_ref):
    acc = jnp.dot(ya_ref[...], wa_ref[...], preferred_element_type=F32)
    acc += jnp.dot(yb_ref[...], wb_ref[...], preferred_element_type=F32)
    o_ref[...] = x_ref[...] + acc


def _mix_out(ya, yb, w, x, *, layer, name):
    m, ka = ya.shape
    kb = yb.shape[1]
    assert ka == kb
    n = w.shape[2]
    tm, tn = _tile(m, 1024), _tile(n, 512)
    return pl.pallas_call(
        _mix_out_kernel,
        out_shape=jax.ShapeDtypeStruct((m, n), F32),
        grid=(m // tm, n // tn),
        in_specs=[pl.BlockSpec((tm, ka), lambda i, j: (i, 0)),
                  pl.BlockSpec((tm, kb), lambda i, j: (i, 0)),
                  pl.BlockSpec((None, ka, tn), lambda i, j: (layer, 0, j)),
                  pl.BlockSpec((None, kb, tn), lambda i, j: (layer, 1, j)),
                  pl.BlockSpec((tm, tn), lambda i, j: (i, j))],
        out_specs=pl.BlockSpec((tm, tn), lambda i, j: (i, j)),
        compiler_params=_params("parallel", "arbitrary"),
        name=name,
    )(ya, yb, w, w, x)


def _down_kernel(a_ref, w_ref, x_ref, o_ref):
    @pl.when(pl.program_id(2) == 0)
    def _():
        o_ref[...] = x_ref[...] + jnp.dot(a_ref[...], w_ref[...], preferred_element_type=F32)

    @pl.when(pl.program_id(2) != 0)
    def _():
        o_ref[...] += jnp.dot(a_ref[...], w_ref[...], preferred_element_type=F32)


def _down_proj(a, w, x, *, layer, name):
    m, k = a.shape
    n = w.shape[2]
    tm, tn = _tile(m, 1024), _tile(n, 1024)
    nk = 4 if (k % (4 * 2 * LANES) == 0 and k > 4096) else 1
    tk = k // nk
    return pl.pallas_call(
        _down_kernel,
        out_shape=jax.ShapeDtypeStruct((m, n), F32),
        grid=(m // tm, n // tn, nk),
        in_specs=[pl.BlockSpec((tm, tk), lambda i, j, kk: (i, kk)),
                  pl.BlockSpec((None, tk, tn), lambda i, j, kk: (layer, kk, j)),
                  pl.BlockSpec((tm, tn), lambda i, j, kk: (i, j))],
        out_specs=pl.BlockSpec((tm, tn), lambda i, j, kk: (i, j)),
        compiler_params=_params("parallel", "arbitrary", "arbitrary"),
        name=name,
    )(a, w, x)


def _gate_kernel(u_ref, v_ref, gv_ref, ws_ref, b_ref, ga_ref, o_ref, mix_ref, *, n_chunks, chunk):
    groups = ws_ref.shape[0]
    vn = _rms(v_ref[...].astype(F32), gv_ref[...]).astype(BF16)
    for c in range(n_chunks):
        rows = slice(c * chunk, (c + 1) * chunk)
        for g in range(groups):
            cols = slice(g * HEAD_DIM, (g + 1) * HEAD_DIM)
            mix_ref[rows, cols] = jnp.dot(ws_ref[g], vn[rows, cols], preferred_element_type=F32)
        mix_ref[rows, :] += b_ref[...]
    ya = u_ref[...].astype(F32) * mix_ref[...]
    o_ref[...] = _rms(ya, ga_ref[...]).astype(o_ref.dtype)


def _spatial_gate(zg, g_v, w_s, b_full, g_a, *, name):
    t, two_gw = zg.shape
    gw = two_gw // 2
    chunk = w_s.shape[1]
    tm = _tile(t, 2 * chunk)
    return pl.pallas_call(
        functools.partial(_gate_kernel, n_chunks=tm // chunk, chunk=chunk),
        out_shape=jax.ShapeDtypeStruct((t, gw), BF16),
        grid=(t // tm,),
        in_specs=[pl.BlockSpec((tm, gw), lambda i: (i, 0)),
                  pl.BlockSpec((tm, gw), lambda i: (i, 1)),
                  pl.BlockSpec((1, gw), lambda i: (0, 0)),
                  pl.BlockSpec(w_s.shape, lambda i: (0, 0, 0)),
                  pl.BlockSpec((chunk, gw), lambda i: (0, 0)),
                  pl.BlockSpec((1, gw), lambda i: (0, 0))],
        out_specs=pl.BlockSpec((tm, gw), lambda i: (i, 0)),
        scratch_shapes=[pltpu.VMEM((tm, gw), F32)],
        compiler_params=_params("parallel"),
        name=name,
    )(zg, zg, g_v.reshape(1, gw), w_s, b_full, g_a.reshape(1, gw))


def _rope(slab, cos_t, sin_t):
    return slab * cos_t + pltpu.roll(slab, QK_ROPE, 1) * sin_t


def _mla_proj_kernel(zc_ref, gq_ref, gkv_ref, wq_ref, wkv_ref, cos_ref, sin_ref,
                     q_ref, k_ref, v_ref, hq_s, hkv_s, kpe_s, *, kv_lora, q_scale):
    heads = q_ref.shape[0]
    cos_t, sin_t = cos_ref[...], sin_ref[...]

    @pl.when(pl.program_id(1) == 0)
    def _():
        zc = zc_ref[...]
        hkv_s[...] = _rms(zc[:, :kv_lora], gkv_ref[...]).astype(BF16)
        kpe_s[...] = _rope(zc[:, kv_lora:kv_lora + LANES], cos_t, sin_t).astype(BF16)
        hq_s[...] = _rms(zc[:, kv_lora + LANES:], gq_ref[...]).astype(BF16)

    q_raw = jnp.dot(hq_s[...], wq_ref[...], preferred_element_type=F32)
    kv_raw = jnp.dot(hkv_s[...], wkv_ref[...], preferred_element_type=F32)
    lane = lax.broadcasted_iota(jnp.int32, (q_raw.shape[0], LANES), 1)
    ones_col = jnp.where(lane == 0, 1.0, 0.0).astype(BF16)
    for h in range(heads):
        base = h * QK_PAD
        q_ref[h, :, :HEAD_DIM] = (q_raw[:, base:base + HEAD_DIM] * q_scale).astype(BF16)
        q_pe = _rope(q_raw[:, base + HEAD_DIM:base + QK_PAD], cos_t, sin_t)
        q_ref[h, :, HEAD_DIM:] = (q_pe * q_scale).astype(BF16)
        k_ref[h, :, :HEAD_DIM] = kv_raw[:, base:base + HEAD_DIM].astype(BF16)
        k_ref[h, :, HEAD_DIM:] = kpe_s[...]
        v_ref[h, :, :HEAD_DIM] = kv_raw[:, base + HEAD_DIM:base + QK_PAD].astype(BF16)
        v_ref[h, :, HEAD_DIM:] = ones_col


def _mla_proj(zc, g_q, g_kv, w_q, w_kv, cos_t, sin_t, *, seq, q_scale, name):
    t, zw = zc.shape
    kv_lora = g_kv.shape[0]
    q_lora = g_q.shape[0]
    heads = w_q.shape[1] // QK_PAD
    hb = _tile(heads, 4)
    tm = _tile(seq, 1024)
    n_pos = seq // tm
    return pl.pallas_call(
        functools.partial(_mla_proj_kernel, kv_lora=kv_lora, q_scale=q_scale),
        out_shape=(jax.ShapeDtypeStruct((heads, t, QK_PAD), BF16),
                   jax.ShapeDtypeStruct((heads, t, QK_PAD), BF16),
                   jax.ShapeDtypeStruct((heads, t, QK_PAD), BF16)),
        grid=(t // tm, heads // hb),
        in_specs=[pl.BlockSpec((tm, zw), lambda i, h: (i, 0)),
                  pl.BlockSpec((1, q_lora), lambda i, h: (0, 0)),
                  pl.BlockSpec((1, kv_lora), lambda i, h: (0, 0)),
                  pl.BlockSpec((q_lora, hb * QK_PAD), lambda i, h: (0, h)),
                  pl.BlockSpec((kv_lora, hb * QK_PAD), lambda i, h: (0, h)),
                  pl.BlockSpec((tm, LANES), lambda i, h: (i % n_pos, 0)),
                  pl.BlockSpec((tm, LANES), lambda i, h: (i % n_pos, 0))],
        out_specs=(pl.BlockSpec((hb, tm, QK_PAD), lambda i, h: (h, i, 0)),
                   pl.BlockSpec((hb, tm, QK_PAD), lambda i, h: (h, i, 0)),
                   pl.BlockSpec((hb, tm, QK_PAD), lambda i, h: (h, i, 0))),
        scratch_shapes=[pltpu.VMEM((tm, q_lora), BF16),
                        pltpu.VMEM((tm, kv_lora), BF16),
                        pltpu.VMEM((tm, LANES), BF16)],
        compiler_params=_params("parallel", "arbitrary"),
        name=name,
    )(zc, g_q.reshape(1, q_lora), g_kv.reshape(1, kv_lora), w_q, w_kv, cos_t, sin_t)


def _attn_kernel(q_ref, k_ref, v_ref, g_ref, o_ref, acc_s, *, tk):
    heads = acc_s.shape[0]
    h = pl.program_id(2)
    q = q_ref[0]
    tq = q.shape[0]
    m = jnp.full((tq, 1), -jnp.inf, F32)
    acc = jnp.zeros((tq, QK_PAD), F32)
    for c in range(k_ref.shape[1] // tk):
        rows = slice(c * tk, (c + 1) * tk)
        s = lax.dot_general(q, k_ref[0, rows, :], _NT_DIMS, preferred_element_type=F32)
        m_new = jnp.maximum(m, jnp.max(s, axis=-1, keepdims=True))
        p = jnp.exp2(s - m_new).astype(BF16)
        pv = jnp.dot(p, v_ref[0, rows, :], preferred_element_type=F32)
        acc = jnp.exp2(m - m_new) * acc + pv
        m = m_new
    acc_s[h] = acc[:, :HEAD_DIM] * (1.0 / acc[:, HEAD_DIM:HEAD_DIM + 1])

    @pl.when(h == heads - 1)
    def _():
        ssq = jnp.zeros((tq, 1), F32)
        for hh in range(heads):
            o = acc_s[hh]
            ssq += jnp.sum(o * o, axis=-1, keepdims=True)
        r = lax.rsqrt(ssq * (1.0 / (heads * HEAD_DIM)) + EPS)
        for hh in range(heads):
            cols = slice(hh * HEAD_DIM, (hh + 1) * HEAD_DIM)
            o_ref[:, cols] = ((acc_s[hh] * r) * g_ref[:, cols]).astype(o_ref.dtype)


def _mla_attention(q, k, v, g_out, *, seq, name):
    heads, t, _ = q.shape
    batch = t // seq
    tq = _tile(seq, 1024)
    tk = _tile(seq, 256)
    nq = seq // tq
    width = heads * HEAD_DIM
    return pl.pallas_call(
        functools.partial(_attn_kernel, tk=tk),
        out_shape=jax.ShapeDtypeStruct((t, width), BF16),
        grid=(batch, nq, heads),
        in_specs=[pl.BlockSpec((1, tq, QK_PAD), lambda b, i, h: (h, b * nq + i, 0)),
                  pl.BlockSpec((1, seq, QK_PAD), lambda b, i, h: (h, b, 0)),
                  pl.BlockSpec((1, seq, QK_PAD), lambda b, i, h: (h, b, 0)),
                  pl.BlockSpec((1, width), lambda b, i, h: (0, 0))],
        out_specs=pl.BlockSpec((tq, width), lambda b, i, h: (b * nq + i, 0)),
        scratch_shapes=[pltpu.VMEM((heads, tq, HEAD_DIM), F32)],
        compiler_params=_params("parallel", "parallel", "arbitrary"),
        name=name,
    )(q, k, v, g_out.reshape(1, width))


def _xa_kv_kernel(mem_ref, g_ref, w_ref, k_ref, v_ref):
    m = _rms(mem_ref[...], g_ref[...]).astype(BF16)
    kv = jnp.dot(m, w_ref[...], preferred_element_type=F32)
    half = kv.shape[1] // 2
    k_ref[...] = kv[:, :half].astype(BF16)
    v_ref[...] = kv[:, half:].astype(BF16)


def _xa_kv(mem, g_mem, w_kv, *, name):
    rows, d = mem.shape
    xw = w_kv.shape[1] // 2
    tm = _tile(rows, 256)
    return pl.pallas_call(
        _xa_kv_kernel,
        out_shape=(jax.ShapeDtypeStruct((rows, xw), BF16), jax.ShapeDtypeStruct((rows, xw), BF16)),
        grid=(rows // tm,),
        in_specs=[pl.BlockSpec((tm, d), lambda i: (i, 0)),
                  pl.BlockSpec((1, d), lambda i: (0, 0)),
                  pl.BlockSpec((d, 2 * xw), lambda i: (0, 0))],
        out_specs=(pl.BlockSpec((tm, xw), lambda i: (i, 0)), pl.BlockSpec((tm, xw), lambda i: (i, 0))),
        compiler_params=_params("parallel"),
        name=name,
    )(mem, g_mem.reshape(1, d), w_kv)


def _xa_kernel(x_ref, gx_ref, wq_ref, k_ref, v_ref, wo_ref, gf_ref, x2_ref, h_ref, o_s, *, scale):
    x = x_ref[...]
    hx = _rms(x, gx_ref[...]).astype(BF16)
    q = (jnp.dot(hx, wq_ref[...], preferred_element_type=F32) * scale).astype(BF16)
    k = k_ref[0]
    v = v_ref[0]
    for hh in range(q.shape[1] // HEAD_DIM):
        cols = slice(hh * HEAD_DIM, (hh + 1) * HEAD_DIM)
        s = lax.dot_general(q[:, cols], k[:, cols], _NT_DIMS, preferred_element_type=F32)
        e = jnp.exp(s - jnp.max(s, axis=-1, keepdims=True))
        p = e * (1.0 / jnp.sum(e, axis=-1, keepdims=True))
        o_s[:, cols] = jnp.dot(p.astype(BF16), v[:, cols], preferred_element_type=F32).astype(BF16)
    x2 = x + jnp.dot(o_s[...], wo_ref[...], preferred_element_type=F32)
    x2_ref[...] = x2
    h_ref[...] = _rms(x2, gf_ref[...]).astype(h_ref.dtype)


def _cross_attention(x, g_xa, w_q, k_mem, v_mem, w_o, g_ffn, *, seq, name):
    t, d = x.shape
    _, n_mem, xw = k_mem.shape
    tm = _tile(seq, 256)
    per_seq = seq // tm
    return pl.pallas_call(
        functools.partial(_xa_kernel, scale=HEAD_DIM ** -0.5),
        out_shape=(jax.ShapeDtypeStruct((t, d), F32), jax.ShapeDtypeStruct((t, d), BF16)),
        grid=(t // tm,),
        in_specs=[pl.BlockSpec((tm, d), lambda i: (i, 0)),
                  pl.BlockSpec((1, d), lambda i: (0, 0)),
                  pl.BlockSpec((d, xw), lambda i: (0, 0)),
                  pl.BlockSpec((1, n_mem, xw), lambda i: (i // per_seq, 0, 0)),
                  pl.BlockSpec((1, n_mem, xw), lambda i: (i // per_seq, 0, 0)),
                  pl.BlockSpec((xw, d), lambda i: (0, 0)),
                  pl.BlockSpec((1, d), lambda i: (0, 0))],
        out_specs=(pl.BlockSpec((tm, d), lambda i: (i, 0)), pl.BlockSpec((tm, d), lambda i: (i, 0))),
        scratch_shapes=[pltpu.VMEM((tm, xw), BF16)],
        compiler_params=_params("parallel"),
        name=name,
    )(x, g_xa.reshape(1, d), w_q, k_mem, v_mem, w_o, g_ffn.reshape(1, d))


def _rope_tables(seq):
    inv = 1.0 / (ROPE_THETA ** (jnp.arange(0, QK_ROPE, 2, dtype=F32) / QK_ROPE))
    ang = jnp.arange(seq, dtype=F32)[:, None] * inv[None, :]
    cos, sin = jnp.cos(ang), jnp.sin(ang)
    zeros = jnp.zeros((seq, LANES - QK_ROPE), F32)
    return (jnp.concatenate([cos, cos, zeros], axis=1),
            jnp.concatenate([-sin, sin, zeros], axis=1))


def _swap_halves(w):
    return jnp.concatenate([w[..., ROPE_HALF:], w[..., :ROPE_HALF]], axis=-1)


def _layout_w_in(w_in, gw, q_lora, kv_lora):
    w_c_q = w_in[:, 2 * gw:2 * gw + q_lora]
    w_c_kv = w_in[:, 2 * gw + q_lora:2 * gw + q_lora + kv_lora]
    w_kr = w_in[:, 2 * gw + q_lora + kv_lora:]
    latent = jnp.concatenate([w_c_kv, w_kr, _swap_halves(w_kr), w_c_q], axis=1)
    return w_in[:, :2 * gw].astype(BF16), latent.astype(BF16)


def _layout_w_uq(w_uq, heads):
    q_lora = w_uq.shape[0]
    w = w_uq.reshape(q_lora, heads, HEAD_DIM + QK_ROPE)
    rope = w[..., HEAD_DIM:]
    w = jnp.concatenate([w[..., :HEAD_DIM], rope, _swap_halves(rope)], axis=-1)
    return w.reshape(q_lora, heads * QK_PAD).astype(BF16)


def _cast_gate_up_kernel(g_ref, u_ref, o_ref, *, d_ff):
    col = lax.broadcasted_iota(jnp.int32, g_ref.shape[1:], 1) + pl.program_id(2) * FF_BLOCK
    valid = col < d_ff
    o_ref[0, :, :FF_BLOCK] = jnp.where(valid, g_ref[0], 0.0).astype(BF16)
    o_ref[0, :, FF_BLOCK:] = jnp.where(valid, u_ref[0], 0.0).astype(BF16)


def _layout_gate_up(w_gate, w_up):
    depth, d, d_ff = w_gate.shape
    n_blocks = -(-d_ff // FF_BLOCK)
    tr = _tile(d, 1024)
    spec = pl.BlockSpec((1, tr, FF_BLOCK), lambda l, r, j: (l, r, j))
    return pl.pallas_call(
        functools.partial(_cast_gate_up_kernel, d_ff=d_ff),
        out_shape=jax.ShapeDtypeStruct((depth, d, 2 * n_blocks * FF_BLOCK), BF16),
        grid=(depth, d // tr, n_blocks),
        in_specs=[spec, spec],
        out_specs=pl.BlockSpec((1, tr, 2 * FF_BLOCK), lambda l, r, j: (l, r, j)),
        compiler_params=_params("parallel", "parallel", "parallel"),
        name="cast_gate_up",
    )(w_gate, w_up)


def _cast_down_kernel(w_ref, o_ref, *, d_ff):
    row = lax.broadcasted_iota(jnp.int32, w_ref.shape[1:], 0) + pl.program_id(1) * FF_BLOCK
    o_ref[0] = jnp.where(row < d_ff, w_ref[0], 0.0).astype(BF16)


def _layout_down(w_down):
    depth, d_ff, d = w_down.shape
    n_blocks = -(-d_ff // FF_BLOCK)
    tc = _tile(d, 2048)
    spec = pl.BlockSpec((1, FF_BLOCK, tc), lambda l, r, c: (l, r, c))
    return pl.pallas_call(
        functools.partial(_cast_down_kernel, d_ff=d_ff),
        out_shape=jax.ShapeDtypeStruct((depth, n_blocks * FF_BLOCK, d), BF16),
        grid=(depth, n_blocks, d // tc),
        in_specs=[spec],
        out_specs=spec,
        compiler_params=_params("parallel", "parallel", "parallel"),
        name="cast_down",
    )(w_down)


def kernel(x_prompt, x_sample, mem_prompt, mem_sample, norm_mix, w_in, sg_norm, sg_w, sg_b, mla_q_norm, mla_w_uq, mla_kv_norm, mla_w_ukv, out_norm_a, out_norm_b, w_out, norm_xa, norm_mem, xa_wq, xa_wk, xa_wv, xa_wo, norm_ffn, w_gate, w_up, w_down, norm_final):
    depth, d = norm_mix.shape
    seq = x_prompt.shape[1]
    assert x_sample.shape[1] == seq and mem_sample.shape[1] == mem_prompt.shape[1]
    gw = sg_norm.shape[1]
    chunk = sg_w.shape[2]
    q_lora, kv_lora = mla_q_norm.shape[1], mla_kv_norm.shape[1]
    heads = out_norm_b.shape[1] // HEAD_DIM
    n_mem = mem_prompt.shape[1]
    assert w_in.shape[2] == 2 * gw + q_lora + kv_lora + QK_ROPE
    assert mla_w_uq.shape[2] == heads * (HEAD_DIM + QK_ROPE) and mla_w_ukv.shape[2] == heads * QK_PAD

    rows_p = x_prompt.shape[0] * seq
    rows_s = x_sample.shape[0] * seq
    x = jnp.concatenate([x_prompt.reshape(rows_p, d), x_sample.reshape(rows_s, d)], axis=0)
    mem = jnp.concatenate([mem_prompt.reshape(-1, d), mem_sample.reshape(-1, d)], axis=0)
    batch = (rows_p + rows_s) // seq

    cos_t, sin_t = _rope_tables(seq)
    q_scale = (HEAD_DIM + QK_ROPE) ** -0.5 * math.log2(math.e)

    w_gu = _layout_gate_up(w_gate, w_up)
    w_dn = _layout_down(w_down)
    w_out_bf = w_out.astype(BF16)

    h_mix = _rmsnorm(x, norm_mix[0], BF16, name="norm_mix0")
    for l in range(depth):
        w_gated, w_latent = _layout_w_in(w_in[l], gw, q_lora, kv_lora)
        zg = _matmul(h_mix, w_gated, act="gelu", out_dtype=BF16, tm=1024, tn=1024, name=f"w_in_gated{l}")
        zc = _matmul(h_mix, w_latent, act=None, out_dtype=F32, tm=1024, tn=512, name=f"w_in_latent{l}")

        b_full = jnp.repeat(sg_b[l].T, HEAD_DIM, axis=1)
        ya = _spatial_gate(zg, sg_norm[l], sg_w[l].astype(BF16), b_full, out_norm_a[l], name=f"gate{l}")

        q, k, v = _mla_proj(zc, mla_q_norm[l], mla_kv_norm[l], _layout_w_uq(mla_w_uq[l], heads),
                            mla_w_ukv[l].astype(BF16), cos_t, sin_t, seq=seq, q_scale=q_scale,
                            name=f"mla_proj{l}")
        yb = _mla_attention(q, k, v, out_norm_b[l], seq=seq, name=f"mla_attn{l}")

        x = _mix_out(ya, yb, w_out_bf, x, layer=l, name=f"w_out{l}")

        w_xkv = jnp.concatenate([xa_wk[l], xa_wv[l]], axis=1).astype(BF16)
        k_mem, v_mem = _xa_kv(mem, norm_mem[l], w_xkv, name=f"xa_kv{l}")
        xw = k_mem.shape[1]
        x, h_ffn = _cross_attention(x, norm_xa[l], xa_wq[l].astype(BF16),
                                    k_mem.reshape(batch, n_mem, xw), v_mem.reshape(batch, n_mem, xw),
                                    xa_wo[l].astype(BF16), norm_ffn[l], seq=seq, name=f"xattn{l}")

        a = _matmul(h_ffn, w_gu, layer=l, act="swiglu", out_dtype=BF16, tm=1024, tn=2 * FF_BLOCK,
                    name=f"ffn_up{l}")
        x = _down_proj(a, w_dn, x, layer=l, name=f"ffn_down{l}")
        if l + 1 < depth:
            h_mix = _rmsnorm(x, norm_mix[l + 1], BF16, name=f"norm_mix{l + 1}")

    y_prompt = _rmsnorm(x, norm_final, F32, row_start=0, rows=rows_p, name="norm_final_prompt")
    y_sample = _rmsnorm(x, norm_final, F32, row_start=rows_p, rows=rows_s, name="norm_final_sample")
    return (y_prompt.reshape(x_prompt.shape), y_sample.reshape(x_sample.shape))
```

```python
import functools
import math

import jax
import jax.numpy as jnp
from jax import lax
from jax.experimental import pallas as pl
from jax.experimental.pallas import tpu as pltpu

F32 = jnp.float32
BF16 = jnp.bfloat16

EPS = 1e-6
ROPE_THETA = 10000.0
HEAD_DIM = 128
QK_ROPE = 64
ROPE_HALF = QK_ROPE // 2
QK_PAD = 2 * HEAD_DIM
LANES = 128
FF_BLOCK = 512
V7X_VMEM_LIMIT_BYTES = 60000 * 1024

_NT_DIMS = (((1,), (1,)), ((), ()))


def _params(*semantics):
    return pltpu.CompilerParams(dimension_semantics=semantics,
                                vmem_limit_bytes=V7X_VMEM_LIMIT_BYTES)


def _tile(n, pref):
    t = min(n, pref)
    while n % t:
        t //= 2
    return t


def _rms(x, g):
    r = lax.rsqrt(jnp.mean(x * x, axis=-1, keepdims=True) + EPS)
    return (x * r) * g


def _norm_kernel(x_ref, g_ref, o_ref):
    o_ref[...] = _rms(x_ref[...], g_ref[...]).astype(o_ref.dtype)


def _rmsnorm(x, g, out_dtype, *, row_start=0, rows=None, name):
    d = x.shape[1]
    rows = x.shape[0] if rows is None else rows
    tm = _tile(math.gcd(rows, row_start) if row_start else rows, 512)
    off = row_start // tm
    return pl.pallas_call(
        _norm_kernel,
        out_shape=jax.ShapeDtypeStruct((rows, d), out_dtype),
        grid=(rows // tm,),
        in_specs=[pl.BlockSpec((tm, d), lambda i: (i + off, 0)),
                  pl.BlockSpec((1, d), lambda i: (0, 0))],
        out_specs=pl.BlockSpec((tm, d), lambda i: (i, 0)),
        compiler_params=_params("parallel"),
        name=name,
    )(x, g.reshape(1, d))


def _norm_latent_kernel(x_ref, g_ref, w_ref, h_ref, z_ref, *, parts):
    step = x_ref.shape[0] // parts
    for part in range(parts):
        rows = slice(part * step, (part + 1) * step)
        h = _rms(x_ref[rows, :], g_ref[...]).astype(BF16)
        h_ref[rows, :] = h
        z_ref[rows, :] = jnp.dot(h, w_ref[...], preferred_element_type=F32)


def _norm_latent(x, g, w, *, name):
    t, d = x.shape
    n = w.shape[1]
    tm = _tile(t, 512)
    parts = 2 if tm % (2 * LANES) == 0 else 1
    return pl.pallas_call(
        functools.partial(_norm_latent_kernel, parts=parts),
        out_shape=(jax.ShapeDtypeStruct((t, d), BF16), jax.ShapeDtypeStruct((t, n), F32)),
        grid=(t // tm,),
        in_specs=[pl.BlockSpec((tm, d), lambda i: (i, 0)),
                  pl.BlockSpec((1, d), lambda i: (0, 0)),
                  pl.BlockSpec((d, n), lambda i: (0, 0), pipeline_mode=pl.Buffered(1))],
        out_specs=(pl.BlockSpec((tm, d), lambda i: (i, 0)), pl.BlockSpec((tm, n), lambda i: (i, 0))),
        compiler_params=_params("parallel"),
        name=name,
    )(x, g.reshape(1, d), w)


def _mm_kernel(x_ref, w_ref, o_ref, *, act):
    acc = jnp.dot(x_ref[...], w_ref[...], preferred_element_type=F32)
    if act == "gelu":
        acc = jax.nn.gelu(acc)
    elif act == "swiglu":
        half = acc.shape[1] // 2
        acc = jax.nn.silu(acc[:, :half]) * acc[:, half:]
    o_ref[...] = acc.astype(o_ref.dtype)


def _matmul(x, w, *, act, out_dtype, tm, tn, name, layer=None):
    m, k = x.shape
    n = w.shape[-1]
    tm, tn = _tile(m, tm), _tile(n, tn)
    out_div = 2 if act == "swiglu" else 1
    if layer is None:
        w_spec = pl.BlockSpec((k, tn), lambda i, j: (0, j))
    else:
        w_spec = pl.BlockSpec((None, k, tn), lambda i, j: (layer, 0, j))
    return pl.pallas_call(
        functools.partial(_mm_kernel, act=act),
        out_shape=jax.ShapeDtypeStruct((m, n // out_div), out_dtype),
        grid=(m // tm, n // tn),
        in_specs=[pl.BlockSpec((tm, k), lambda i, j: (i, 0)), w_spec],
        out_specs=pl.BlockSpec((tm, tn // out_div), lambda i, j: (i, j)),
        compiler_params=_params("parallel", "arbitrary"),
        name=name,
    )(x, w)


def _mix_out_kernel(ya_ref, yb_ref, wa_ref, wb_ref, x_ref, o_ref):
    acc = jnp.dot(ya_ref[...], wa_ref[...], preferred_element_type=F32)
    acc += jnp.dot(yb_ref[...], wb_ref[...], preferred_element_type=F32)
    o_ref[...] = x_ref[...] + acc


def _mix_out(ya, yb, w, x, *, layer, name):
    m, ka = ya.shape
    kb = yb.shape[1]
    assert ka == kb
    n = w.shape[2]
    tm, tn = _tile(m, 1024), _tile(n, 1024)
    return pl.pallas_call(
        _mix_out_kernel,
        out_shape=jax.ShapeDtypeStruct((m, n), F32),
        grid=(m // tm, n // tn),
        in_specs=[pl.BlockSpec((tm, ka), lambda i, j: (i, 0)),
                  pl.BlockSpec((tm, kb), lambda i, j: (i, 0)),
                  pl.BlockSpec((None, ka, tn), lambda i, j: (layer, 0, j)),
                  pl.BlockSpec((None, kb, tn), lambda i, j: (layer, 1, j)),
                  pl.BlockSpec((tm, tn), lambda i, j: (i, j))],
        out_specs=pl.BlockSpec((tm, tn), lambda i, j: (i, j)),
        compiler_params=_params("parallel", "arbitrary"),
        name=name,
    )(ya, yb, w, w, x)


def _down_kernel(a_ref, w_ref, x_ref, o_ref):
    @pl.when(pl.program_id(2) == 0)
    def _():
        o_ref[...] = x_ref[...] + jnp.dot(a_ref[...], w_ref[...], preferred_element_type=F32)

    @pl.when(pl.program_id(2) != 0)
    def _():
        o_ref[...] += jnp.dot(a_ref[...], w_ref[...], preferred_element_type=F32)


def _down_proj(a, w, x, *, layer, name):
    m, k = a.shape
    n = w.shape[2]
    tm, tn = _tile(m, 1024), _tile(n, 1024)
    nk = 4 if (k % (4 * 2 * LANES) == 0 and k > 4096) else 1
    tk = k // nk
    return pl.pallas_call(
        _down_kernel,
        out_shape=jax.ShapeDtypeStruct((m, n), F32),
        grid=(m // tm, n // tn, nk),
        in_specs=[pl.BlockSpec((tm, tk), lambda i, j, kk: (i, kk)),
                  pl.BlockSpec((None, tk, tn), lambda i, j, kk: (layer, kk, j)),
                  pl.BlockSpec((tm, tn), lambda i, j, kk: (i, j))],
        out_specs=pl.BlockSpec((tm, tn), lambda i, j, kk: (i, j)),
        compiler_params=_params("parallel", "arbitrary", "arbitrary"),
        name=name,
    )(a, w, x)


def _gate_kernel(u_ref, v_ref, gv_ref, ws_ref, b_ref, ga_ref, o_ref, mix_ref, *, n_chunks, chunk):
    groups = ws_ref.shape[0]
    vn = _rms(v_ref[...].astype(F32), gv_ref[...]).astype(BF16)
    for c in range(n_chunks):
        rows = slice(c * chunk, (c + 1) * chunk)
        for g in range(groups):
            cols = slice(g * HEAD_DIM, (g + 1) * HEAD_DIM)
            mix_ref[rows, cols] = jnp.dot(ws_ref[g], vn[rows, cols], preferred_element_type=F32)
        mix_ref[rows, :] += b_ref[...]
    ya = u_ref[...].astype(F32) * mix_ref[...]
    o_ref[...] = _rms(ya, ga_ref[...]).astype(o_ref.dtype)


def _spatial_gate(zg, g_v, w_s, b_full, g_a, *, name):
    t, two_gw = zg.shape
    gw = two_gw // 2
    chunk = w_s.shape[1]
    tm = _tile(t, 2 * chunk)
    return pl.pallas_call(
        functools.partial(_gate_kernel, n_chunks=tm // chunk, chunk=chunk),
        out_shape=jax.ShapeDtypeStruct((t, gw), BF16),
        grid=(t // tm,),
        in_specs=[pl.BlockSpec((tm, gw), lambda i: (i, 0)),
                  pl.BlockSpec((tm, gw), lambda i: (i, 1)),
                  pl.BlockSpec((1, gw), lambda i: (0, 0)),
                  pl.BlockSpec(w_s.shape, lambda i: (0, 0, 0)),
                  pl.BlockSpec((chunk, gw), lambda i: (0, 0)),
                  pl.BlockSpec((1, gw), lambda i: (0, 0))],
        out_specs=pl.BlockSpec((tm, gw), lambda i: (i, 0)),
        scratch_shapes=[pltpu.VMEM((tm, gw), F32)],
        compiler_params=_params("parallel"),
        name=name,
    )(zg, zg, g_v.reshape(1, gw), w_s, b_full, g_a.reshape(1, gw))


def _rope(slab, cos_t, sin_t):
    return slab * cos_t + pltpu.roll(slab, QK_ROPE, 1) * sin_t


def _mla_proj_kernel(zc_ref, gq_ref, gkv_ref, wq_ref, wkv_ref, cos_ref, sin_ref,
                     q_ref, k_ref, v_ref, hq_s, hkv_s, kpe_s, *, kv_lora, q_scale):
    heads = q_ref.shape[0]
    cos_t, sin_t = cos_ref[...], sin_ref[...]

    @pl.when(pl.program_id(1) == 0)
    def _():
        zc = zc_ref[...]
        hkv_s[...] = _rms(zc[:, :kv_lora], gkv_ref[...]).astype(BF16)
        kpe_s[...] = _rope(zc[:, kv_lora:kv_lora + LANES], cos_t, sin_t).astype(BF16)
        hq_s[...] = _rms(zc[:, kv_lora + LANES:], gq_ref[...]).astype(BF16)

    q_raw = jnp.dot(hq_s[...], wq_ref[...], preferred_element_type=F32)
    kv_raw = jnp.dot(hkv_s[...], wkv_ref[...], preferred_element_type=F32)
    lane = lax.broadcasted_iota(jnp.int32, (q_raw.shape[0], LANES), 1)
    ones_col = jnp.where(lane == 0, 1.0, 0.0).astype(BF16)
    for h in range(heads):
        base = h * QK_PAD
        q_ref[h, :, :HEAD_DIM] = (q_raw[:, base:base + HEAD_DIM] * q_scale).astype(BF16)
        q_pe = _rope(q_raw[:, base + HEAD_DIM:base + QK_PAD], cos_t, sin_t)
        q_ref[h, :, HEAD_DIM:] = (q_pe * q_scale).astype(BF16)
        k_ref[h, :, :HEAD_DIM] = kv_raw[:, base:base + HEAD_DIM].astype(BF16)
        k_ref[h, :, HEAD_DIM:] = kpe_s[...]
        v_ref[h, :, :HEAD_DIM] = kv_raw[:, base + HEAD_DIM:base + QK_PAD].astype(BF16)
        v_ref[h, :, HEAD_DIM:] = ones_col


def _mla_proj(zc, g_q, g_kv, w_q, w_kv, cos_t, sin_t, *, seq, q_scale, name):
    t, zw = zc.shape
    kv_lora = g_kv.shape[0]
    q_lora = g_q.shape[0]
    heads = w_q.shape[1] // QK_PAD
    hb = _tile(heads, 4)
    tm = _tile(seq, 1024)
    n_pos = seq // tm
    return pl.pallas_call(
        functools.partial(_mla_proj_kernel, kv_lora=kv_lora, q_scale=q_scale),
        out_shape=(jax.ShapeDtypeStruct((heads, t, QK_PAD), BF16),
                   jax.ShapeDtypeStruct((heads, t, QK_PAD), BF16),
                   jax.ShapeDtypeStruct((heads, t, QK_PAD), BF16)),
        grid=(t // tm, heads // hb),
        in_specs=[pl.BlockSpec((tm, zw), lambda i, h: (i, 0)),
                  pl.BlockSpec((1, q_lora), lambda i, h: (0, 0)),
                  pl.BlockSpec((1, kv_lora), lambda i, h: (0, 0)),
                  pl.BlockSpec((q_lora, hb * QK_PAD), lambda i, h: (0, h)),
                  pl.BlockSpec((kv_lora, hb * QK_PAD), lambda i, h: (0, h)),
                  pl.BlockSpec((tm, LANES), lambda i, h: (i % n_pos, 0)),
                  pl.BlockSpec((tm, LANES), lambda i, h: (i % n_pos, 0))],
        out_specs=(pl.BlockSpec((hb, tm, QK_PAD), lambda i, h: (h, i, 0)),
                   pl.BlockSpec((hb, tm, QK_PAD), lambda i, h: (h, i, 0)),
                   pl.BlockSpec((hb, tm, QK_PAD), lambda i, h: (h, i, 0))),
        scratch_shapes=[pltpu.VMEM((tm, q_lora), BF16),
                        pltpu.VMEM((tm, kv_lora), BF16),
                        pltpu.VMEM((tm, LANES), BF16)],
        compiler_params=_params("parallel", "arbitrary"),
        name=name,
    )(zc, g_q.reshape(1, q_lora), g_kv.reshape(1, kv_lora), w_q, w_kv, cos_t, sin_t)


def _attn_kernel(q_ref, k_ref, v_ref, g_ref, o_ref, acc_s, *, tk):
    heads = acc_s.shape[0]
    h = pl.program_id(2)
    q = q_ref[0]
    tq = q.shape[0]
    m = jnp.full((tq, 1), -jnp.inf, F32)
    acc = jnp.zeros((tq, QK_PAD), F32)
    for c in range(k_ref.shape[1] // tk):
        rows = slice(c * tk, (c + 1) * tk)
        s = lax.dot_general(q, k_ref[0, rows, :], _NT_DIMS, preferred_element_type=F32)
        m_new = jnp.maximum(m, jnp.max(s, axis=-1, keepdims=True))
        p = jnp.exp2(s - m_new).astype(BF16)
        pv = jnp.dot(p, v_ref[0, rows, :], preferred_element_type=F32)
        acc = jnp.exp2(m - m_new) * acc + pv
        m = m_new
    acc_s[h] = acc[:, :HEAD_DIM] * (1.0 / acc[:, HEAD_DIM:HEAD_DIM + 1])

    @pl.when(h == heads - 1)
    def _():
        ssq = jnp.zeros((tq, 1), F32)
        for hh in range(heads):
            o = acc_s[hh]
            ssq += jnp.sum(o * o, axis=-1, keepdims=True)
        r = lax.rsqrt(ssq * (1.0 / (heads * HEAD_DIM)) + EPS)
        for hh in range(heads):
            cols = slice(hh * HEAD_DIM, (hh + 1) * HEAD_DIM)
            o_ref[:, cols] = ((acc_s[hh] * r) * g_ref[:, cols]).astype(o_ref.dtype)


def _mla_attention(q, k, v, g_out, *, seq, name):
    heads, t, _ = q.shape
    batch = t // seq
    tq = _tile(seq, 1024)
    tk = _tile(seq, 2 * LANES)
    nq = seq // tq
    width = heads * HEAD_DIM
    return pl.pallas_call(
        functools.partial(_attn_kernel, tk=tk),
        out_shape=jax.ShapeDtypeStruct((t, width), BF16),
        grid=(batch, nq, heads),
        in_specs=[pl.BlockSpec((1, tq, QK_PAD), lambda b, i, h: (h, b * nq + i, 0)),
                  pl.BlockSpec((1, seq, QK_PAD), lambda b, i, h: (h, b, 0)),
                  pl.BlockSpec((1, seq, QK_PAD), lambda b, i, h: (h, b, 0)),
                  pl.BlockSpec((1, width), lambda b, i, h: (0, 0))],
        out_specs=pl.BlockSpec((tq, width), lambda b, i, h: (b * nq + i, 0)),
        scratch_shapes=[pltpu.VMEM((heads, tq, HEAD_DIM), F32)],
        compiler_params=_params("parallel", "parallel", "arbitrary"),
        name=name,
    )(q, k, v, g_out.reshape(1, width))


def _xa_kv_kernel(mem_ref, g_ref, w_ref, k_ref, v_ref):
    m = _rms(mem_ref[...], g_ref[...]).astype(BF16)
    kv = jnp.dot(m, w_ref[...], preferred_element_type=F32)
    half = kv.shape[1] // 2
    k_ref[...] = kv[:, :half].astype(BF16)
    v_ref[...] = kv[:, half:].astype(BF16)


def _xa_kv(mem, g_mem, w_kv, *, name):
    rows, d = mem.shape
    xw = w_kv.shape[1] // 2
    tm = _tile(rows, 256)
    return pl.pallas_call(
        _xa_kv_kernel,
        out_shape=(jax.ShapeDtypeStruct((rows, xw), BF16), jax.ShapeDtypeStruct((rows, xw), BF16)),
        grid=(rows // tm,),
        in_specs=[pl.BlockSpec((tm, d), lambda i: (i, 0)),
                  pl.BlockSpec((1, d), lambda i: (0, 0)),
                  pl.BlockSpec((d, 2 * xw), lambda i: (0, 0))],
        out_specs=(pl.BlockSpec((tm, xw), lambda i: (i, 0)), pl.BlockSpec((tm, xw), lambda i: (i, 0))),
        compiler_params=_params("parallel"),
        name=name,
    )(mem, g_mem.reshape(1, d), w_kv)


def _xa_kernel(x_ref, gx_ref, wq_ref, k_ref, v_ref, wo_ref, gf_ref, x2_ref, h_ref, o_s, *, scale, parts):
    k = k_ref[0]
    v = v_ref[0]
    step = x_ref.shape[0] // parts
    for part in range(parts):
        rows = slice(part * step, (part + 1) * step)
        x = x_ref[rows, :]
        hx = _rms(x, gx_ref[...]).astype(BF16)
        q = (jnp.dot(hx, wq_ref[...], preferred_element_type=F32) * scale).astype(BF16)
        for hh in range(q.shape[1] // HEAD_DIM):
            cols = slice(hh * HEAD_DIM, (hh + 1) * HEAD_DIM)
            s = lax.dot_general(q[:, cols], k[:, cols], _NT_DIMS, preferred_element_type=F32)
            e = jnp.exp(s - jnp.max(s, axis=-1, keepdims=True))
            p = e * (1.0 / jnp.sum(e, axis=-1, keepdims=True))
            o_s[rows, cols] = jnp.dot(p.astype(BF16), v[:, cols], preferred_element_type=F32).astype(BF16)
        x2 = x + jnp.dot(o_s[rows, :], wo_ref[...], preferred_element_type=F32)
        x2_ref[rows, :] = x2
        h_ref[rows, :] = _rms(x2, gf_ref[...]).astype(h_ref.dtype)


def _cross_attention(x, g_xa, w_q, k_mem, v_mem, w_o, g_ffn, *, seq, name):
    t, d = x.shape
    _, n_mem, xw = k_mem.shape
    tm = _tile(seq, 512)
    parts = 2 if tm % (2 * LANES) == 0 else 1
    per_seq = seq // tm
    once = pl.Buffered(1)
    return pl.pallas_call(
        functools.partial(_xa_kernel, scale=HEAD_DIM ** -0.5, parts=parts),
        out_shape=(jax.ShapeDtypeStruct((t, d), F32), jax.ShapeDtypeStruct((t, d), BF16)),
        grid=(t // tm,),
        in_specs=[pl.BlockSpec((tm, d), lambda i: (i, 0)),
                  pl.BlockSpec((1, d), lambda i: (0, 0)),
                  pl.BlockSpec((d, xw), lambda i: (0, 0), pipeline_mode=once),
                  pl.BlockSpec((1, n_mem, xw), lambda i: (i // per_seq, 0, 0)),
                  pl.BlockSpec((1, n_mem, xw), lambda i: (i // per_seq, 0, 0)),
                  pl.BlockSpec((xw, d), lambda i: (0, 0), pipeline_mode=once),
                  pl.BlockSpec((1, d), lambda i: (0, 0))],
        out_specs=(pl.BlockSpec((tm, d), lambda i: (i, 0)), pl.BlockSpec((tm, d), lambda i: (i, 0))),
        scratch_shapes=[pltpu.VMEM((tm, xw), BF16)],
        compiler_params=_params("parallel"),
        name=name,
    )(x, g_xa.reshape(1, d), w_q, k_mem, v_mem, w_o, g_ffn.reshape(1, d))


def _rope_tables(seq):
    inv = 1.0 / (ROPE_THETA ** (jnp.arange(0, QK_ROPE, 2, dtype=F32) / QK_ROPE))
    ang = jnp.arange(seq, dtype=F32)[:, None] * inv[None, :]
    cos, sin = jnp.cos(ang), jnp.sin(ang)
    zeros = jnp.zeros((seq, LANES - QK_ROPE), F32)
    return (jnp.concatenate([cos, cos, zeros], axis=1),
            jnp.concatenate([-sin, sin, zeros], axis=1))


def _swap_halves(w):
    return jnp.concatenate([w[..., ROPE_HALF:], w[..., :ROPE_HALF]], axis=-1)


def _layout_w_in(w_in, gw, q_lora, kv_lora):
    w_c_q = w_in[:, 2 * gw:2 * gw + q_lora]
    w_c_kv = w_in[:, 2 * gw + q_lora:2 * gw + q_lora + kv_lora]
    w_kr = w_in[:, 2 * gw + q_lora + kv_lora:]
    latent = jnp.concatenate([w_c_kv, w_kr, _swap_halves(w_kr), w_c_q], axis=1)
    return w_in[:, :2 * gw].astype(BF16), latent.astype(BF16)


def _layout_w_uq(w_uq, heads):
    q_lora = w_uq.shape[0]
    w = w_uq.reshape(q_lora, heads, HEAD_DIM + QK_ROPE)
    rope = w[..., HEAD_DIM:]
    w = jnp.concatenate([w[..., :HEAD_DIM], rope, _swap_halves(rope)], axis=-1)
    return w.reshape(q_lora, heads * QK_PAD).astype(BF16)


def _cast_gate_up_kernel(g_ref, u_ref, o_ref, *, d_ff):
    col = lax.broadcasted_iota(jnp.int32, g_ref.shape[1:], 1) + pl.program_id(2) * FF_BLOCK
    valid = col < d_ff
    o_ref[0, :, :FF_BLOCK] = jnp.where(valid, g_ref[0], 0.0).astype(BF16)
    o_ref[0, :, FF_BLOCK:] = jnp.where(valid, u_ref[0], 0.0).astype(BF16)


def _layout_gate_up(w_gate, w_up):
    depth, d, d_ff = w_gate.shape
    n_blocks = -(-d_ff // FF_BLOCK)
    tr = _tile(d, 1024)
    spec = pl.BlockSpec((1, tr, FF_BLOCK), lambda l, r, j: (l, r, j))
    return pl.pallas_call(
        functools.partial(_cast_gate_up_kernel, d_ff=d_ff),
        out_shape=jax.ShapeDtypeStruct((depth, d, 2 * n_blocks * FF_BLOCK), BF16),
        grid=(depth, d // tr, n_blocks),
        in_specs=[spec, spec],
        out_specs=pl.BlockSpec((1, tr, 2 * FF_BLOCK), lambda l, r, j: (l, r, j)),
        compiler_params=_params("parallel", "parallel", "parallel"),
        name="cast_gate_up",
    )(w_gate, w_up)


def _cast_down_kernel(w_ref, o_ref, *, d_ff):
    row = lax.broadcasted_iota(jnp.int32, w_ref.shape[1:], 0) + pl.program_id(1) * FF_BLOCK
    o_ref[0] = jnp.where(row < d_ff, w_ref[0], 0.0).astype(BF16)


def _layout_down(w_down):
    depth, d_ff, d = w_down.shape
    n_blocks = -(-d_ff // FF_BLOCK)
    tc = _tile(d, 2048)
    spec = pl.BlockSpec((1, FF_BLOCK, tc), lambda l, r, c: (l, r, c))
    return pl.pallas_call(
        functools.partial(_cast_down_kernel, d_ff=d_ff),
        out_shape=jax.ShapeDtypeStruct((depth, n_blocks * FF_BLOCK, d), BF16),
        grid=(depth, n_blocks, d // tc),
        in_specs=[spec],
        out_specs=spec,
        compiler_params=_params("parallel", "parallel", "parallel"),
        name="cast_down",
    )(w_down)


def kernel(x_prompt, x_sample, mem_prompt, mem_sample, norm_mix, w_in, sg_norm, sg_w, sg_b, mla_q_norm, mla_w_uq, mla_kv_norm, mla_w_ukv, out_norm_a, out_norm_b, w_out, norm_xa, norm_mem, xa_wq, xa_wk, xa_wv, xa_wo, norm_ffn, w_gate, w_up, w_down, norm_final):
    depth, d = norm_mix.shape
    seq = x_prompt.shape[1]
    assert x_sample.shape[1] == seq and mem_sample.shape[1] == mem_prompt.shape[1]
    gw = sg_norm.shape[1]
    chunk = sg_w.shape[2]
    q_lora, kv_lora = mla_q_norm.shape[1], mla_kv_norm.shape[1]
    heads = out_norm_b.shape[1] // HEAD_DIM
    n_mem = mem_prompt.shape[1]
    assert w_in.shape[2] == 2 * gw + q_lora + kv_lora + QK_ROPE
    assert mla_w_uq.shape[2] == heads * (HEAD_DIM + QK_ROPE) and mla_w_ukv.shape[2] == heads * QK_PAD

    rows_p = x_prompt.shape[0] * seq
    rows_s = x_sample.shape[0] * seq
    x = jnp.concatenate([x_prompt.reshape(rows_p, d), x_sample.reshape(rows_s, d)], axis=0)
    mem = jnp.concatenate([mem_prompt.reshape(-1, d), mem_sample.reshape(-1, d)], axis=0)
    batch = (rows_p + rows_s) // seq

    cos_t, sin_t = _rope_tables(seq)
    q_scale = (HEAD_DIM + QK_ROPE) ** -0.5 * math.log2(math.e)

    w_gu = _layout_gate_up(w_gate, w_up)
    w_dn = _layout_down(w_down)
    w_out_bf = w_out.astype(BF16)

    for l in range(depth):
        w_gated, w_latent = _layout_w_in(w_in[l], gw, q_lora, kv_lora)
        h_mix, zc = _norm_latent(x, norm_mix[l], w_latent, name=f"w_in_latent{l}")
        zg = _matmul(h_mix, w_gated, act="gelu", out_dtype=BF16, tm=1024, tn=1024, name=f"w_in_gated{l}")

        b_full = jnp.repeat(sg_b[l].T, HEAD_DIM, axis=1)
        ya = _spatial_gate(zg, sg_norm[l], sg_w[l].astype(BF16), b_full, out_norm_a[l], name=f"gate{l}")

        q, k, v = _mla_proj(zc, mla_q_norm[l], mla_kv_norm[l], _layout_w_uq(mla_w_uq[l], heads),
                            mla_w_ukv[l].astype(BF16), cos_t, sin_t, seq=seq, q_scale=q_scale,
                            name=f"mla_proj{l}")
        yb = _mla_attention(q, k, v, out_norm_b[l], seq=seq, name=f"mla_attn{l}")

        x = _mix_out(ya, yb, w_out_bf, x, layer=l, name=f"w_out{l}")

        w_xkv = jnp.concatenate([xa_wk[l], xa_wv[l]], axis=1).astype(BF16)
        k_mem, v_mem = _xa_kv(mem, norm_mem[l], w_xkv, name=f"xa_kv{l}")
        xw = k_mem.shape[1]
        x, h_ffn = _cross_attention(x, norm_xa[l], xa_wq[l].astype(BF16),
                                    k_mem.reshape(batch, n_mem, xw), v_mem.reshape(batch, n_mem, xw),
                                    xa_wo[l].astype(BF16), norm_ffn[l], seq=seq, name=f"xattn{l}")

        a = _matmul(h_ffn, w_gu, layer=l, act="swiglu", out_dtype=BF16, tm=1024, tn=2 * FF_BLOCK,
                    name=f"ffn_up{l}")
        x = _down_proj(a, w_dn, x, layer=l, name=f"ffn_down{l}")

    y_prompt = _rmsnorm(x, norm_final, F32, row_start=0, rows=rows_p, name="norm_final_prompt")
    y_sample = _rmsnorm(x, norm_final, F32, row_start=rows_p, rows=rows_s, name="norm_final_sample")
    return (y_prompt.reshape(x_prompt.shape), y_sample.reshape(x_sample.shape))
```

```python
import functools
import math

import jax
import jax.numpy as jnp
from jax import lax
from jax.experimental import pallas as pl
from jax.experimental.pallas import tpu as pltpu

F32 = jnp.float32
BF16 = jnp.bfloat16

EPS = 1e-6
ROPE_THETA = 10000.0
HEAD_DIM = 128
QK_ROPE = 64
ROPE_HALF = QK_ROPE // 2
QK_PAD = 2 * HEAD_DIM
LANES = 128
FF_GRANULE = 256
FF_BLOCK = 512
FF_K_SPLIT = 4
V7X_VMEM_LIMIT_BYTES = 60000 * 1024

_NT_DIMS = (((1,), (1,)), ((), ()))


def _params(*semantics):
    return pltpu.CompilerParams(dimension_semantics=semantics,
                                vmem_limit_bytes=V7X_VMEM_LIMIT_BYTES)


def _tile(n, pref):
    t = min(n, pref)
    while n % t:
        t //= 2
    return t


def _rms(x, g):
    r = lax.rsqrt(jnp.mean(x * x, axis=-1, keepdims=True) + EPS)
    return (x * r) * g


def _norm_kernel(x_ref, g_ref, o_ref):
    o_ref[...] = _rms(x_ref[...], g_ref[...]).astype(o_ref.dtype)


def _rmsnorm(x, g, out_dtype, *, row_start=0, rows=None, name):
    d = x.shape[1]
    rows = x.shape[0] if rows is None else rows
    tm = _tile(math.gcd(rows, row_start) if row_start else rows, 512)
    off = row_start // tm
    return pl.pallas_call(
        _norm_kernel,
        out_shape=jax.ShapeDtypeStruct((rows, d), out_dtype),
        grid=(rows // tm,),
        in_specs=[pl.BlockSpec((tm, d), lambda i: (i + off, 0)),
                  pl.BlockSpec((1, d), lambda i: (0, 0))],
        out_specs=pl.BlockSpec((tm, d), lambda i: (i, 0)),
        compiler_params=_params("parallel"),
        name=name,
    )(x, g.reshape(1, d))


def _norm_latent_kernel(x_ref, g_ref, w_ref, h_ref, z_ref, *, parts):
    step = x_ref.shape[0] // parts
    for part in range(parts):
        rows = slice(part * step, (part + 1) * step)
        h = _rms(x_ref[rows, :], g_ref[...]).astype(BF16)
        h_ref[rows, :] = h
        z_ref[rows, :] = jnp.dot(h, w_ref[...], preferred_element_type=F32)


def _norm_latent(x, g, w, *, name):
    t, d = x.shape
    n = w.shape[1]
    tm = _tile(t, 512)
    parts = 2 if tm % (2 * LANES) == 0 else 1
    return pl.pallas_call(
        functools.partial(_norm_latent_kernel, parts=parts),
        out_shape=(jax.ShapeDtypeStruct((t, d), BF16), jax.ShapeDtypeStruct((t, n), F32)),
        grid=(t // tm,),
        in_specs=[pl.BlockSpec((tm, d), lambda i: (i, 0)),
                  pl.BlockSpec((1, d), lambda i: (0, 0)),
                  pl.BlockSpec((d, n), lambda i: (0, 0), pipeline_mode=pl.Buffered(1))],
        out_specs=(pl.BlockSpec((tm, d), lambda i: (i, 0)), pl.BlockSpec((tm, n), lambda i: (i, 0))),
        compiler_params=_params("parallel"),
        name=name,
    )(x, g.reshape(1, d), w)


def _gelu_mm_kernel(x_ref, w_ref, o_ref):
    acc = jnp.dot(x_ref[...], w_ref[...], preferred_element_type=F32)
    o_ref[...] = jax.nn.gelu(acc).astype(o_ref.dtype)


def _gelu_matmul(x, w, *, name):
    m, k = x.shape
    n = w.shape[1]
    tm, tn = _tile(m, 1024), _tile(n, 1024)
    return pl.pallas_call(
        _gelu_mm_kernel,
        out_shape=jax.ShapeDtypeStruct((m, n), BF16),
        grid=(m // tm, n // tn),
        in_specs=[pl.BlockSpec((tm, k), lambda i, j: (i, 0)),
                  pl.BlockSpec((k, tn), lambda i, j: (0, j))],
        out_specs=pl.BlockSpec((tm, tn), lambda i, j: (i, j)),
        compiler_params=_params("parallel", "arbitrary"),
        name=name,
    )(x, w)


def _mix_out_kernel(ya_ref, yb_ref, wa_ref, wb_ref, x_ref, o_ref):
    acc = jnp.dot(ya_ref[...], wa_ref[...], preferred_element_type=F32)
    acc += jnp.dot(yb_ref[...], wb_ref[...], preferred_element_type=F32)
    o_ref[...] = x_ref[...] + acc


def _mix_out(ya, yb, w, x, *, layer, name):
    m, ka = ya.shape
    kb = yb.shape[1]
    assert ka == kb
    n = w.shape[2]
    tm, tn = _tile(m, 1024), _tile(n, 1024)
    return pl.pallas_call(
        _mix_out_kernel,
        out_shape=jax.ShapeDtypeStruct((m, n), F32),
        grid=(m // tm, n // tn),
        in_specs=[pl.BlockSpec((tm, ka), lambda i, j: (i, 0)),
                  pl.BlockSpec((tm, kb), lambda i, j: (i, 0)),
                  pl.BlockSpec((None, ka, tn), lambda i, j: (layer, 0, j)),
                  pl.BlockSpec((None, kb, tn), lambda i, j: (layer, 1, j)),
                  pl.BlockSpec((tm, tn), lambda i, j: (i, j))],
        out_specs=pl.BlockSpec((tm, tn), lambda i, j: (i, j)),
        compiler_params=_params("parallel", "arbitrary"),
        name=name,
    )(ya, yb, w, w, x)


def _down_kernel(a_ref, w_ref, x_ref, o_ref):
    @pl.when(pl.program_id(2) == 0)
    def _():
        o_ref[...] = x_ref[...] + jnp.dot(a_ref[...], w_ref[...], preferred_element_type=F32)

    @pl.when(pl.program_id(2) != 0)
    def _():
        o_ref[...] += jnp.dot(a_ref[...], w_ref[...], preferred_element_type=F32)


def _down_proj(a, w, x, *, layer, name):
    m, k = a.shape
    n = w.shape[2]
    tm, tn = _tile(m, 1024), _tile(n, 1024)
    assert k % (FF_K_SPLIT * 2 * LANES) == 0
    nk = FF_K_SPLIT if k > 4096 else 1
    tk = k // nk
    return pl.pallas_call(
        _down_kernel,
        out_shape=jax.ShapeDtypeStruct((m, n), F32),
        grid=(m // tm, n // tn, nk),
        in_specs=[pl.BlockSpec((tm, tk), lambda i, j, kk: (i, kk)),
                  pl.BlockSpec((None, tk, tn), lambda i, j, kk: (layer, kk, j)),
                  pl.BlockSpec((tm, tn), lambda i, j, kk: (i, j))],
        out_specs=pl.BlockSpec((tm, tn), lambda i, j, kk: (i, j)),
        compiler_params=_params("parallel", "arbitrary", "arbitrary"),
        name=name,
    )(a, w, x)


def _gate_kernel(u_ref, v_ref, gv_ref, ws_ref, b_ref, ga_ref, o_ref, mix_ref, *, n_chunks, chunk):
    groups = ws_ref.shape[0]
    vn = _rms(v_ref[...].astype(F32), gv_ref[...]).astype(BF16)
    for c in range(n_chunks):
        rows = slice(c * chunk, (c + 1) * chunk)
        for g in range(groups):
            cols = slice(g * HEAD_DIM, (g + 1) * HEAD_DIM)
            mix_ref[rows, cols] = jnp.dot(ws_ref[g], vn[rows, cols], preferred_element_type=F32)
        mix_ref[rows, :] += b_ref[...]
    ya = u_ref[...].astype(F32) * mix_ref[...]
    o_ref[...] = _rms(ya, ga_ref[...]).astype(o_ref.dtype)


def _spatial_gate(zg, g_v, w_s, b_full, g_a, *, name):
    t, two_gw = zg.shape
    gw = two_gw // 2
    chunk = w_s.shape[1]
    tm = _tile(t, 2 * chunk)
    return pl.pallas_call(
        functools.partial(_gate_kernel, n_chunks=tm // chunk, chunk=chunk),
        out_shape=jax.ShapeDtypeStruct((t, gw), BF16),
        grid=(t // tm,),
        in_specs=[pl.BlockSpec((tm, gw), lambda i: (i, 0)),
                  pl.BlockSpec((tm, gw), lambda i: (i, 1)),
                  pl.BlockSpec((1, gw), lambda i: (0, 0)),
                  pl.BlockSpec(w_s.shape, lambda i: (0, 0, 0)),
                  pl.BlockSpec((chunk, gw), lambda i: (0, 0)),
                  pl.BlockSpec((1, gw), lambda i: (0, 0))],
        out_specs=pl.BlockSpec((tm, gw), lambda i: (i, 0)),
        scratch_shapes=[pltpu.VMEM((tm, gw), F32)],
        compiler_params=_params("parallel"),
        name=name,
    )(zg, zg, g_v.reshape(1, gw), w_s, b_full, g_a.reshape(1, gw))


def _rope(slab, cos_t, sin_t):
    return slab * cos_t + pltpu.roll(slab, QK_ROPE, 1) * sin_t


def _mla_proj_kernel(zc_ref, gq_ref, gkv_ref, wq_ref, wkv_ref, cos_ref, sin_ref,
                     q_ref, k_ref, v_ref, hq_s, hkv_s, kpe_s, *, kv_lora, q_scale):
    heads = q_ref.shape[0]
    cos_t, sin_t = cos_ref[...], sin_ref[...]

    @pl.when(pl.program_id(1) == 0)
    def _():
        zc = zc_ref[...]
        hkv_s[...] = _rms(zc[:, :kv_lora], gkv_ref[...]).astype(BF16)
        kpe_s[...] = _rope(zc[:, kv_lora:kv_lora + LANES], cos_t, sin_t).astype(BF16)
        hq_s[...] = _rms(zc[:, kv_lora + LANES:], gq_ref[...]).astype(BF16)

    q_raw = jnp.dot(hq_s[...], wq_ref[...], preferred_element_type=F32)
    kv_raw = jnp.dot(hkv_s[...], wkv_ref[...], preferred_element_type=F32)
    lane = lax.broadcasted_iota(jnp.int32, (q_raw.shape[0], LANES), 1)
    ones_col = jnp.where(lane == 0, 1.0, 0.0).astype(BF16)
    for h in range(heads):
        base = h * QK_PAD
        q_ref[h, :, :HEAD_DIM] = (q_raw[:, base:base + HEAD_DIM] * q_scale).astype(BF16)
        q_pe = _rope(q_raw[:, base + HEAD_DIM:base + QK_PAD], cos_t, sin_t)
        q_ref[h, :, HEAD_DIM:] = (q_pe * q_scale).astype(BF16)
        k_ref[h, :, :HEAD_DIM] = kv_raw[:, base:base + HEAD_DIM].astype(BF16)
        k_ref[h, :, HEAD_DIM:] = kpe_s[...]
        v_ref[h, :, :HEAD_DIM] = kv_raw[:, base + HEAD_DIM:base + QK_PAD].astype(BF16)
        v_ref[h, :, HEAD_DIM:] = ones_col


def _mla_proj(zc, g_q, g_kv, w_q, w_kv, cos_t, sin_t, *, seq, q_scale, name):
    t, zw = zc.shape
    kv_lora = g_kv.shape[0]
    q_lora = g_q.shape[0]
    heads = w_q.shape[1] // QK_PAD
    hb = _tile(heads, 4)
    tm = _tile(seq, 1024)
    n_pos = seq // tm
    return pl.pallas_call(
        functools.partial(_mla_proj_kernel, kv_lora=kv_lora, q_scale=q_scale),
        out_shape=(jax.ShapeDtypeStruct((heads, t, QK_PAD), BF16),
                   jax.ShapeDtypeStruct((heads, t, QK_PAD), BF16),
                   jax.ShapeDtypeStruct((heads, t, QK_PAD), BF16)),
        grid=(t // tm, heads // hb),
        in_specs=[pl.BlockSpec((tm, zw), lambda i, h: (i, 0)),
                  pl.BlockSpec((1, q_lora), lambda i, h: (0, 0)),
                  pl.BlockSpec((1, kv_lora), lambda i, h: (0, 0)),
                  pl.BlockSpec((q_lora, hb * QK_PAD), lambda i, h: (0, h)),
                  pl.BlockSpec((kv_lora, hb * QK_PAD), lambda i, h: (0, h)),
                  pl.BlockSpec((tm, LANES), lambda i, h: (i % n_pos, 0)),
                  pl.BlockSpec((tm, LANES), lambda i, h: (i % n_pos, 0))],
        out_specs=(pl.BlockSpec((hb, tm, QK_PAD), lambda i, h: (h, i, 0)),
                   pl.BlockSpec((hb, tm, QK_PAD), lambda i, h: (h, i, 0)),
                   pl.BlockSpec((hb, tm, QK_PAD), lambda i, h: (h, i, 0))),
        scratch_shapes=[pltpu.VMEM((tm, q_lora), BF16),
                        pltpu.VMEM((tm, kv_lora), BF16),
                        pltpu.VMEM((tm, LANES), BF16)],
        compiler_params=_params("parallel", "arbitrary"),
        name=name,
    )(zc, g_q.reshape(1, q_lora), g_kv.reshape(1, kv_lora), w_q, w_kv, cos_t, sin_t)


def _attn_kernel(q_ref, k_ref, v_ref, g_ref, o_ref, acc_s, *, tk):
    heads = acc_s.shape[0]
    h = pl.program_id(2)
    q = q_ref[0]
    tq = q.shape[0]
    m = jnp.full((tq, 1), -jnp.inf, F32)
    acc = jnp.zeros((tq, QK_PAD), F32)
    for c in range(k_ref.shape[1] // tk):
        rows = slice(c * tk, (c + 1) * tk)
        s = lax.dot_general(q, k_ref[0, rows, :], _NT_DIMS, preferred_element_type=F32)
        m_new = jnp.maximum(m, jnp.max(s, axis=-1, keepdims=True))
        p = jnp.exp2(s - m_new).astype(BF16)
        pv = jnp.dot(p, v_ref[0, rows, :], preferred_element_type=F32)
        acc = jnp.exp2(m - m_new) * acc + pv
        m = m_new
    acc_s[h] = acc[:, :HEAD_DIM] * (1.0 / acc[:, HEAD_DIM:HEAD_DIM + 1])

    @pl.when(h == heads - 1)
    def _():
        ssq = jnp.zeros((tq, 1), F32)
        for hh in range(heads):
            o = acc_s[hh]
            ssq += jnp.sum(o * o, axis=-1, keepdims=True)
        r = lax.rsqrt(ssq * (1.0 / (heads * HEAD_DIM)) + EPS)
        for hh in range(heads):
            cols = slice(hh * HEAD_DIM, (hh + 1) * HEAD_DIM)
            o_ref[:, cols] = ((acc_s[hh] * r) * g_ref[:, cols]).astype(o_ref.dtype)


def _mla_attention(q, k, v, g_out, *, seq, name):
    heads, t, _ = q.shape
    batch = t // seq
    tq = _tile(seq, 1024)
    tk = _tile(seq, 2 * LANES)
    nq = seq // tq
    width = heads * HEAD_DIM
    return pl.pallas_call(
        functools.partial(_attn_kernel, tk=tk),
        out_shape=jax.ShapeDtypeStruct((t, width), BF16),
        grid=(batch, nq, heads),
        in_specs=[pl.BlockSpec((1, tq, QK_PAD), lambda b, i, h: (h, b * nq + i, 0)),
                  pl.BlockSpec((1, seq, QK_PAD), lambda b, i, h: (h, b, 0)),
                  pl.BlockSpec((1, seq, QK_PAD), lambda b, i, h: (h, b, 0)),
                  pl.BlockSpec((1, width), lambda b, i, h: (0, 0))],
        out_specs=pl.BlockSpec((tq, width), lambda b, i, h: (b * nq + i, 0)),
        scratch_shapes=[pltpu.VMEM((heads, tq, HEAD_DIM), F32)],
        compiler_params=_params("parallel", "parallel", "arbitrary"),
        name=name,
    )(q, k, v, g_out.reshape(1, width))


def _xa_kv_kernel(mem_ref, g_ref, w_ref, k_ref, v_ref):
    m = _rms(mem_ref[...], g_ref[...]).astype(BF16)
    kv = jnp.dot(m, w_ref[...], preferred_element_type=F32)
    half = kv.shape[1] // 2
    k_ref[...] = kv[:, :half].astype(BF16)
    v_ref[...] = kv[:, half:].astype(BF16)


def _xa_kv(mem, g_mem, w_kv, *, name):
    rows, d = mem.shape
    xw = w_kv.shape[1] // 2
    tm = _tile(rows, 256)
    return pl.pallas_call(
        _xa_kv_kernel,
        out_shape=(jax.ShapeDtypeStruct((rows, xw), BF16), jax.ShapeDtypeStruct((rows, xw), BF16)),
        grid=(rows // tm,),
        in_specs=[pl.BlockSpec((tm, d), lambda i: (i, 0)),
                  pl.BlockSpec((1, d), lambda i: (0, 0)),
                  pl.BlockSpec((d, 2 * xw), lambda i: (0, 0))],
        out_specs=(pl.BlockSpec((tm, xw), lambda i: (i, 0)), pl.BlockSpec((tm, xw), lambda i: (i, 0))),
        compiler_params=_params("parallel"),
        name=name,
    )(mem, g_mem.reshape(1, d), w_kv)


def _xa_kernel(x_ref, gx_ref, wq_ref, k_ref, v_ref, wo_ref, gf_ref, x2_ref, h_ref, o_s, *, scale, parts):
    k = k_ref[0]
    v = v_ref[0]
    step = x_ref.shape[0] // parts
    for part in range(parts):
        rows = slice(part * step, (part + 1) * step)
        x = x_ref[rows, :]
        hx = _rms(x, gx_ref[...]).astype(BF16)
        q = (jnp.dot(hx, wq_ref[...], preferred_element_type=F32) * scale).astype(BF16)
        for hh in range(q.shape[1] // HEAD_DIM):
            cols = slice(hh * HEAD_DIM, (hh + 1) * HEAD_DIM)
            s = lax.dot_general(q[:, cols], k[:, cols], _NT_DIMS, preferred_element_type=F32)
            e = jnp.exp(s - jnp.max(s, axis=-1, keepdims=True))
            p = e * (1.0 / jnp.sum(e, axis=-1, keepdims=True))
            o_s[rows, cols] = jnp.dot(p.astype(BF16), v[:, cols], preferred_element_type=F32).astype(BF16)
        x2 = x + jnp.dot(o_s[rows, :], wo_ref[...], preferred_element_type=F32)
        x2_ref[rows, :] = x2
        h_ref[rows, :] = _rms(x2, gf_ref[...]).astype(h_ref.dtype)


def _cross_attention(x, g_xa, w_q, k_mem, v_mem, w_o, g_ffn, *, seq, name):
    t, d = x.shape
    _, n_mem, xw = k_mem.shape
    tm = _tile(seq, 512)
    parts = 2 if tm % (2 * LANES) == 0 else 1
    per_seq = seq // tm
    once = pl.Buffered(1)
    return pl.pallas_call(
        functools.partial(_xa_kernel, scale=HEAD_DIM ** -0.5, parts=parts),
        out_shape=(jax.ShapeDtypeStruct((t, d), F32), jax.ShapeDtypeStruct((t, d), BF16)),
        grid=(t // tm,),
        in_specs=[pl.BlockSpec((tm, d), lambda i: (i, 0)),
                  pl.BlockSpec((1, d), lambda i: (0, 0)),
                  pl.BlockSpec((d, xw), lambda i: (0, 0), pipeline_mode=once),
                  pl.BlockSpec((1, n_mem, xw), lambda i: (i // per_seq, 0, 0)),
                  pl.BlockSpec((1, n_mem, xw), lambda i: (i // per_seq, 0, 0)),
                  pl.BlockSpec((xw, d), lambda i: (0, 0), pipeline_mode=once),
                  pl.BlockSpec((1, d), lambda i: (0, 0))],
        out_specs=(pl.BlockSpec((tm, d), lambda i: (i, 0)), pl.BlockSpec((tm, d), lambda i: (i, 0))),
        scratch_shapes=[pltpu.VMEM((tm, xw), BF16)],
        compiler_params=_params("parallel"),
        name=name,
    )(x, g_xa.reshape(1, d), w_q, k_mem, v_mem, w_o, g_ffn.reshape(1, d))


def _rope_tables(seq):
    inv = 1.0 / (ROPE_THETA ** (jnp.arange(0, QK_ROPE, 2, dtype=F32) / QK_ROPE))
    ang = jnp.arange(seq, dtype=F32)[:, None] * inv[None, :]
    cos, sin = jnp.cos(ang), jnp.sin(ang)
    zeros = jnp.zeros((seq, LANES - QK_ROPE), F32)
    return (jnp.concatenate([cos, cos, zeros], axis=1),
            jnp.concatenate([-sin, sin, zeros], axis=1))


def _swap_halves(w):
    return jnp.concatenate([w[..., ROPE_HALF:], w[..., :ROPE_HALF]], axis=-1)


def _layout_w_in(w_in, gw, q_lora, kv_lora):
    w_c_q = w_in[:, 2 * gw:2 * gw + q_lora]
    w_c_kv = w_in[:, 2 * gw + q_lora:2 * gw + q_lora + kv_lora]
    w_kr = w_in[:, 2 * gw + q_lora + kv_lora:]
    latent = jnp.concatenate([w_c_kv, w_kr, _swap_halves(w_kr), w_c_q], axis=1)
    return w_in[:, :2 * gw].astype(BF16), latent.astype(BF16)


def _layout_w_uq(w_uq, heads):
    q_lora = w_uq.shape[0]
    w = w_uq.reshape(q_lora, heads, HEAD_DIM + QK_ROPE)
    rope = w[..., HEAD_DIM:]
    w = jnp.concatenate([w[..., :HEAD_DIM], rope, _swap_halves(rope)], axis=-1)
    return w.reshape(q_lora, heads * QK_PAD).astype(BF16)


def _cast_gate_up_kernel(g_ref, u_ref, o_ref):
    for p in range(g_ref.shape[2] // FF_GRANULE):
        src = slice(p * FF_GRANULE, (p + 1) * FF_GRANULE)
        o_ref[0, :, 2 * p * FF_GRANULE:(2 * p + 1) * FF_GRANULE] = g_ref[0, :, src].astype(BF16)
        o_ref[0, :, (2 * p + 1) * FF_GRANULE:(2 * p + 2) * FF_GRANULE] = u_ref[0, :, src].astype(BF16)


def _layout_gate_up(w_gate, w_up):
    depth, d, d_ff = w_gate.shape
    assert d_ff % FF_GRANULE == 0
    tr = _tile(d, LANES)
    spec = pl.BlockSpec((1, tr, d_ff), lambda l, r: (l, r, 0))
    return pl.pallas_call(
        _cast_gate_up_kernel,
        out_shape=jax.ShapeDtypeStruct((depth, d, 2 * d_ff), BF16),
        grid=(depth, d // tr),
        in_specs=[spec, spec],
        out_specs=pl.BlockSpec((1, tr, 2 * d_ff), lambda l, r: (l, r, 0)),
        compiler_params=_params("parallel", "parallel"),
        name="cast_gate_up",
    )(w_gate, w_up)


def _ffn_up_kernel(x_ref, w_ref, o_ref, *, n_real):
    @pl.when(pl.program_id(1) < n_real)
    def _():
        half = x_ref.shape[0] // 2
        for rows in (slice(0, half), slice(half, 2 * half)):
            acc = jnp.dot(x_ref[rows, :], w_ref[...], preferred_element_type=F32)
            o_ref[rows, :] = (jax.nn.silu(acc[:, :FF_GRANULE]) * acc[:, FF_GRANULE:]).astype(o_ref.dtype)

    @pl.when(pl.program_id(1) >= n_real)
    def _():
        o_ref[...] = jnp.zeros_like(o_ref)


def _ffn_up(h, w_gu, *, layer, name):
    m, k = h.shape
    d_ff = w_gu.shape[2] // 2
    n_real = d_ff // FF_GRANULE
    n_blocks = _ff_pad(d_ff) // FF_GRANULE
    tm = _tile(m, 2048)
    return pl.pallas_call(
        functools.partial(_ffn_up_kernel, n_real=n_real),
        out_shape=jax.ShapeDtypeStruct((m, n_blocks * FF_GRANULE), BF16),
        grid=(m // tm, n_blocks),
        in_specs=[pl.BlockSpec((tm, k), lambda i, j: (i, 0)),
                  pl.BlockSpec((None, k, 2 * FF_GRANULE),
                               lambda i, j: (layer, 0, jnp.minimum(j, n_real - 1)))],
        out_specs=pl.BlockSpec((tm, FF_GRANULE), lambda i, j: (i, j)),
        compiler_params=_params("parallel", "arbitrary"),
        name=name,
    )(h, w_gu)


def _ff_pad(d_ff):
    unit = FF_K_SPLIT * 2 * LANES
    return -(-d_ff // unit) * unit


def _cast_down_kernel(w_ref, o_ref, *, d_ff):
    row = lax.broadcasted_iota(jnp.int32, w_ref.shape[1:], 0) + pl.program_id(1) * FF_BLOCK
    o_ref[0] = jnp.where(row < d_ff, w_ref[0], 0.0).astype(BF16)


def _layout_down(w_down):
    depth, d_ff, d = w_down.shape
    n_blocks = _ff_pad(d_ff) // FF_BLOCK
    assert n_blocks * FF_BLOCK - d_ff < FF_BLOCK
    tc = _tile(d, 2048)
    spec = pl.BlockSpec((1, FF_BLOCK, tc), lambda l, r, c: (l, r, c))
    return pl.pallas_call(
        functools.partial(_cast_down_kernel, d_ff=d_ff),
        out_shape=jax.ShapeDtypeStruct((depth, n_blocks * FF_BLOCK, d), BF16),
        grid=(depth, n_blocks, d // tc),
        in_specs=[spec],
        out_specs=spec,
        compiler_params=_params("parallel", "parallel", "parallel"),
        name="cast_down",
    )(w_down)


def kernel(x_prompt, x_sample, mem_prompt, mem_sample, norm_mix, w_in, sg_norm, sg_w, sg_b, mla_q_norm, mla_w_uq, mla_kv_norm, mla_w_ukv, out_norm_a, out_norm_b, w_out, norm_xa, norm_mem, xa_wq, xa_wk, xa_wv, xa_wo, norm_ffn, w_gate, w_up, w_down, norm_final):
    depth, d = norm_mix.shape
    seq = x_prompt.shape[1]
    assert x_sample.shape[1] == seq and mem_sample.shape[1] == mem_prompt.shape[1]
    gw = sg_norm.shape[1]
    chunk = sg_w.shape[2]
    q_lora, kv_lora = mla_q_norm.shape[1], mla_kv_norm.shape[1]
    heads = out_norm_b.shape[1] // HEAD_DIM
    n_mem = mem_prompt.shape[1]
    assert w_in.shape[2] == 2 * gw + q_lora + kv_lora + QK_ROPE
    assert mla_w_uq.shape[2] == heads * (HEAD_DIM + QK_ROPE) and mla_w_ukv.shape[2] == heads * QK_PAD

    rows_p = x_prompt.shape[0] * seq
    rows_s = x_sample.shape[0] * seq
    x = jnp.concatenate([x_prompt.reshape(rows_p, d), x_sample.reshape(rows_s, d)], axis=0)
    mem = jnp.concatenate([mem_prompt.reshape(-1, d), mem_sample.reshape(-1, d)], axis=0)
    batch = (rows_p + rows_s) // seq

    cos_t, sin_t = _rope_tables(seq)
    q_scale = (HEAD_DIM + QK_ROPE) ** -0.5 * math.log2(math.e)

    w_gu = _layout_gate_up(w_gate, w_up)
    w_dn = _layout_down(w_down)
    w_out_bf = w_out.astype(BF16)

    for l in range(depth):
        w_gated, w_latent = _layout_w_in(w_in[l], gw, q_lora, kv_lora)
        h_mix, zc = _norm_latent(x, norm_mix[l], w_latent, name=f"w_in_latent{l}")
        zg = _gelu_matmul(h_mix, w_gated, name=f"w_in_gated{l}")

        b_full = jnp.repeat(sg_b[l].T, HEAD_DIM, axis=1)
        ya = _spatial_gate(zg, sg_norm[l], sg_w[l].astype(BF16), b_full, out_norm_a[l], name=f"gate{l}")

        q, k, v = _mla_proj(zc, mla_q_norm[l], mla_kv_norm[l], _layout_w_uq(mla_w_uq[l], heads),
                            mla_w_ukv[l].astype(BF16), cos_t, sin_t, seq=seq, q_scale=q_scale,
                            name=f"mla_proj{l}")
        yb = _mla_attention(q, k, v, out_norm_b[l], seq=seq, name=f"mla_attn{l}")

        x = _mix_out(ya, yb, w_out_bf, x, layer=l, name=f"w_out{l}")

        w_xkv = jnp.concatenate([xa_wk[l], xa_wv[l]], axis=1).astype(BF16)
        k_mem, v_mem = _xa_kv(mem, norm_mem[l], w_xkv, name=f"xa_kv{l}")
        xw = k_mem.shape[1]
        x, h_ffn = _cross_attention(x, norm_xa[l], xa_wq[l].astype(BF16),
                                    k_mem.reshape(batch, n_mem, xw), v_mem.reshape(batch, n_mem, xw),
                                    xa_wo[l].astype(BF16), norm_ffn[l], seq=seq, name=f"xattn{l}")

        a = _ffn_up(h_ffn, w_gu, layer=l, name=f"ffn_up{l}")
        x = _down_proj(a, w_dn, x, layer=l, name=f"ffn_down{l}")

    y_prompt = _rmsnorm(x, norm_final, F32, row_start=0, rows=rows_p, name="norm_final_prompt")
    y_sample = _rmsnorm(x, norm_final, F32, row_start=rows_p, rows=rows_s, name="norm_final_sample")
    return (y_prompt.reshape(x_prompt.shape), y_sample.reshape(x_sample.shape))
```

```python
import functools
import math

import jax
import jax.numpy as jnp
from jax import lax
from jax.experimental import pallas as pl
from jax.experimental.pallas import tpu as pltpu

F32 = jnp.float32
BF16 = jnp.bfloat16

EPS = 1e-6
ROPE_THETA = 10000.0
HEAD_DIM = 128
QK_ROPE = 64
ROPE_HALF = QK_ROPE // 2
QK_PAD = 2 * HEAD_DIM
LANES = 128
FF_GRANULE = 256
FF_BLOCK = 512
FF_K_SPLIT = 4
V7X_VMEM_LIMIT_BYTES = 60000 * 1024

_NT_DIMS = (((1,), (1,)), ((), ()))


def _params(*semantics):
    return pltpu.CompilerParams(dimension_semantics=semantics,
                                vmem_limit_bytes=V7X_VMEM_LIMIT_BYTES)


def _tile(n, pref):
    t = min(n, pref)
    while n % t:
        t //= 2
    return t


def _rms(x, g):
    r = lax.rsqrt(jnp.mean(x * x, axis=-1, keepdims=True) + EPS)
    return (x * r) * g


def _norm_kernel(x_ref, g_ref, o_ref):
    o_ref[...] = _rms(x_ref[...], g_ref[...]).astype(o_ref.dtype)


def _rmsnorm(x, g, out_dtype, *, row_start=0, rows=None, name):
    d = x.shape[1]
    rows = x.shape[0] if rows is None else rows
    tm = _tile(math.gcd(rows, row_start) if row_start else rows, 512)
    off = row_start // tm
    return pl.pallas_call(
        _norm_kernel,
        out_shape=jax.ShapeDtypeStruct((rows, d), out_dtype),
        grid=(rows // tm,),
        in_specs=[pl.BlockSpec((tm, d), lambda i: (i + off, 0)),
                  pl.BlockSpec((1, d), lambda i: (0, 0))],
        out_specs=pl.BlockSpec((tm, d), lambda i: (i, 0)),
        compiler_params=_params("parallel"),
        name=name,
    )(x, g.reshape(1, d))


def _norm_latent_kernel(x_ref, g_ref, w_ref, h_ref, z_ref, *, parts):
    step = x_ref.shape[0] // parts
    for part in range(parts):
        rows = slice(part * step, (part + 1) * step)
        h = _rms(x_ref[rows, :], g_ref[...]).astype(BF16)
        h_ref[rows, :] = h
        z_ref[rows, :] = jnp.dot(h, w_ref[...], preferred_element_type=F32)


def _norm_latent(x, g, w, *, name):
    t, d = x.shape
    n = w.shape[1]
    tm = _tile(t, 512)
    parts = 2 if tm % (2 * LANES) == 0 else 1
    return pl.pallas_call(
        functools.partial(_norm_latent_kernel, parts=parts),
        out_shape=(jax.ShapeDtypeStruct((t, d), BF16), jax.ShapeDtypeStruct((t, n), F32)),
        grid=(t // tm,),
        in_specs=[pl.BlockSpec((tm, d), lambda i: (i, 0)),
                  pl.BlockSpec((1, d), lambda i: (0, 0)),
                  pl.BlockSpec((d, n), lambda i: (0, 0), pipeline_mode=pl.Buffered(1))],
        out_specs=(pl.BlockSpec((tm, d), lambda i: (i, 0)), pl.BlockSpec((tm, n), lambda i: (i, 0))),
        compiler_params=_params("parallel"),
        name=name,
    )(x, g.reshape(1, d), w)


def _gelu_mm_kernel(x_ref, w_ref, o_ref):
    acc = jnp.dot(x_ref[...], w_ref[...], preferred_element_type=F32)
    o_ref[...] = jax.nn.gelu(acc).astype(o_ref.dtype)


def _gelu_matmul(x, w, *, name):
    m, k = x.shape
    n = w.shape[1]
    tm, tn = _tile(m, 1024), _tile(n, 1024)
    return pl.pallas_call(
        _gelu_mm_kernel,
        out_shape=jax.ShapeDtypeStruct((m, n), BF16),
        grid=(m // tm, n // tn),
        in_specs=[pl.BlockSpec((tm, k), lambda i, j: (i, 0)),
                  pl.BlockSpec((k, tn), lambda i, j: (0, j))],
        out_specs=pl.BlockSpec((tm, tn), lambda i, j: (i, j)),
        compiler_params=_params("parallel", "arbitrary"),
        name=name,
    )(x, w)


def _mix_out_kernel(ya_ref, yb_ref, wa_ref, wb_ref, x_ref, o_ref):
    acc = jnp.dot(ya_ref[...], wa_ref[...], preferred_element_type=F32)
    acc += jnp.dot(yb_ref[...], wb_ref[...], preferred_element_type=F32)
    o_ref[...] = x_ref[...] + acc


def _mix_out(ya, yb, w, x, *, layer, name):
    m, ka = ya.shape
    kb = yb.shape[1]
    assert ka == kb
    n = w.shape[2]
    tm, tn = _tile(m, 1024), _tile(n, 1024)
    return pl.pallas_call(
        _mix_out_kernel,
        out_shape=jax.ShapeDtypeStruct((m, n), F32),
        grid=(m // tm, n // tn),
        in_specs=[pl.BlockSpec((tm, ka), lambda i, j: (i, 0)),
                  pl.BlockSpec((tm, kb), lambda i, j: (i, 0)),
                  pl.BlockSpec((None, ka, tn), lambda i, j: (layer, 0, j)),
                  pl.BlockSpec((None, kb, tn), lambda i, j: (layer, 1, j)),
                  pl.BlockSpec((tm, tn), lambda i, j: (i, j))],
        out_specs=pl.BlockSpec((tm, tn), lambda i, j: (i, j)),
        compiler_params=_params("parallel", "arbitrary"),
        name=name,
    )(ya, yb, w, w, x)


def _down_kernel(a_ref, w_ref, x_ref, o_ref):
    @pl.when(pl.program_id(2) == 0)
    def _():
        o_ref[...] = x_ref[...] + jnp.dot(a_ref[...], w_ref[...], preferred_element_type=F32)

    @pl.when(pl.program_id(2) != 0)
    def _():
        o_ref[...] += jnp.dot(a_ref[...], w_ref[...], preferred_element_type=F32)


def _down_proj(a, w, x, *, name):
    m, k = a.shape
    n = w.shape[1]
    tm, tn = _tile(m, 1024), _tile(n, 1024)
    assert k % (FF_K_SPLIT * 2 * LANES) == 0
    nk = FF_K_SPLIT if k > 4096 else 1
    tk = k // nk
    return pl.pallas_call(
        _down_kernel,
        out_shape=jax.ShapeDtypeStruct((m, n), F32),
        grid=(m // tm, n // tn, nk),
        in_specs=[pl.BlockSpec((tm, tk), lambda i, j, kk: (i, kk)),
                  pl.BlockSpec((tk, tn), lambda i, j, kk: (kk, j)),
                  pl.BlockSpec((tm, tn), lambda i, j, kk: (i, j))],
        out_specs=pl.BlockSpec((tm, tn), lambda i, j, kk: (i, j)),
        compiler_params=_params("parallel", "arbitrary", "arbitrary"),
        name=name,
    )(a, w, x)


def _gate_kernel(u_ref, v_ref, gv_ref, ws_ref, b_ref, ga_ref, o_ref, mix_ref, *, n_chunks, chunk):
    groups = ws_ref.shape[0]
    vn = _rms(v_ref[...].astype(F32), gv_ref[...]).astype(BF16)
    for c in range(n_chunks):
        rows = slice(c * chunk, (c + 1) * chunk)
        for g in range(groups):
            cols = slice(g * HEAD_DIM, (g + 1) * HEAD_DIM)
            mix_ref[rows, cols] = jnp.dot(ws_ref[g], vn[rows, cols], preferred_element_type=F32)
        mix_ref[rows, :] += b_ref[...]
    ya = u_ref[...].astype(F32) * mix_ref[...]
    o_ref[...] = _rms(ya, ga_ref[...]).astype(o_ref.dtype)


def _spatial_gate(zg, g_v, w_s, b_full, g_a, *, name):
    t, two_gw = zg.shape
    gw = two_gw // 2
    chunk = w_s.shape[1]
    tm = _tile(t, 2 * chunk)
    return pl.pallas_call(
        functools.partial(_gate_kernel, n_chunks=tm // chunk, chunk=chunk),
        out_shape=jax.ShapeDtypeStruct((t, gw), BF16),
        grid=(t // tm,),
        in_specs=[pl.BlockSpec((tm, gw), lambda i: (i, 0)),
                  pl.BlockSpec((tm, gw), lambda i: (i, 1)),
                  pl.BlockSpec((1, gw), lambda i: (0, 0)),
                  pl.BlockSpec(w_s.shape, lambda i: (0, 0, 0)),
                  pl.BlockSpec((chunk, gw), lambda i: (0, 0)),
                  pl.BlockSpec((1, gw), lambda i: (0, 0))],
        out_specs=pl.BlockSpec((tm, gw), lambda i: (i, 0)),
        scratch_shapes=[pltpu.VMEM((tm, gw), F32)],
        compiler_params=_params("parallel"),
        name=name,
    )(zg, zg, g_v.reshape(1, gw), w_s, b_full, g_a.reshape(1, gw))


def _rope(slab, cos_t, sin_t):
    return slab * cos_t + pltpu.roll(slab, QK_ROPE, 1) * sin_t


def _mla_proj_kernel(zc_ref, gq_ref, gkv_ref, wq_ref, wkv_ref, cos_ref, sin_ref,
                     q_ref, k_ref, v_ref, hq_s, hkv_s, kpe_s, *, kv_lora, q_scale):
    heads = q_ref.shape[0]
    cos_t, sin_t = cos_ref[...], sin_ref[...]

    @pl.when(pl.program_id(1) == 0)
    def _():
        zc = zc_ref[...]
        hkv_s[...] = _rms(zc[:, :kv_lora], gkv_ref[...]).astype(BF16)
        kpe_s[...] = _rope(zc[:, kv_lora:kv_lora + LANES], cos_t, sin_t).astype(BF16)
        hq_s[...] = _rms(zc[:, kv_lora + LANES:], gq_ref[...]).astype(BF16)

    q_raw = jnp.dot(hq_s[...], wq_ref[...], preferred_element_type=F32)
    kv_raw = jnp.dot(hkv_s[...], wkv_ref[...], preferred_element_type=F32)
    lane = lax.broadcasted_iota(jnp.int32, (q_raw.shape[0], LANES), 1)
    ones_col = jnp.where(lane == 0, 1.0, 0.0).astype(BF16)
    for h in range(heads):
        base = h * QK_PAD
        q_ref[h, :, :HEAD_DIM] = (q_raw[:, base:base + HEAD_DIM] * q_scale).astype(BF16)
        q_pe = _rope(q_raw[:, base + HEAD_DIM:base + QK_PAD], cos_t, sin_t)
        q_ref[h, :, HEAD_DIM:] = (q_pe * q_scale).astype(BF16)
        k_ref[h, :, :HEAD_DIM] = kv_raw[:, base:base + HEAD_DIM].astype(BF16)
        k_ref[h, :, HEAD_DIM:] = kpe_s[...]
        v_ref[h, :, :HEAD_DIM] = kv_raw[:, base + HEAD_DIM:base + QK_PAD].astype(BF16)
        v_ref[h, :, HEAD_DIM:] = ones_col


def _mla_proj(zc, g_q, g_kv, w_q, w_kv, cos_t, sin_t, *, seq, q_scale, name):
    t, zw = zc.shape
    kv_lora = g_kv.shape[0]
    q_lora = g_q.shape[0]
    heads = w_q.shape[1] // QK_PAD
    hb = _tile(heads, 4)
    tm = _tile(seq, 1024)
    n_pos = seq // tm
    return pl.pallas_call(
        functools.partial(_mla_proj_kernel, kv_lora=kv_lora, q_scale=q_scale),
        out_shape=(jax.ShapeDtypeStruct((heads, t, QK_PAD), BF16),
                   jax.ShapeDtypeStruct((heads, t, QK_PAD), BF16),
                   jax.ShapeDtypeStruct((heads, t, QK_PAD), BF16)),
        grid=(t // tm, heads // hb),
        in_specs=[pl.BlockSpec((tm, zw), lambda i, h: (i, 0)),
                  pl.BlockSpec((1, q_lora), lambda i, h: (0, 0)),
                  pl.BlockSpec((1, kv_lora), lambda i, h: (0, 0)),
                  pl.BlockSpec((q_lora, hb * QK_PAD), lambda i, h: (0, h)),
                  pl.BlockSpec((kv_lora, hb * QK_PAD), lambda i, h: (0, h)),
                  pl.BlockSpec((tm, LANES), lambda i, h: (i % n_pos, 0)),
                  pl.BlockSpec((tm, LANES), lambda i, h: (i % n_pos, 0))],
        out_specs=(pl.BlockSpec((hb, tm, QK_PAD), lambda i, h: (h, i, 0)),
                   pl.BlockSpec((hb, tm, QK_PAD), lambda i, h: (h, i, 0)),
                   pl.BlockSpec((hb, tm, QK_PAD), lambda i, h: (h, i, 0))),
        scratch_shapes=[pltpu.VMEM((tm, q_lora), BF16),
                        pltpu.VMEM((tm, kv_lora), BF16),
                        pltpu.VMEM((tm, LANES), BF16)],
        compiler_params=_params("parallel", "arbitrary"),
        name=name,
    )(zc, g_q.reshape(1, q_lora), g_kv.reshape(1, kv_lora), w_q, w_kv, cos_t, sin_t)


def _attn_kernel(q_ref, k_ref, v_ref, g_ref, o_ref, acc_s, *, tk):
    heads = acc_s.shape[0]
    h = pl.program_id(2)
    q = q_ref[0]
    tq = q.shape[0]
    m = jnp.full((tq, 1), -jnp.inf, F32)
    acc = jnp.zeros((tq, QK_PAD), F32)
    for c in range(k_ref.shape[1] // tk):
        rows = slice(c * tk, (c + 1) * tk)
        s = lax.dot_general(q, k_ref[0, rows, :], _NT_DIMS, preferred_element_type=F32)
        m_new = jnp.maximum(m, jnp.max(s, axis=-1, keepdims=True))
        p = jnp.exp2(s - m_new).astype(BF16)
        pv = jnp.dot(p, v_ref[0, rows, :], preferred_element_type=F32)
        acc = jnp.exp2(m - m_new) * acc + pv
        m = m_new
    acc_s[h] = acc[:, :HEAD_DIM] * (1.0 / acc[:, HEAD_DIM:HEAD_DIM + 1])

    @pl.when(h == heads - 1)
    def _():
        ssq = jnp.zeros((tq, 1), F32)
        for hh in range(heads):
            o = acc_s[hh]
            ssq += jnp.sum(o * o, axis=-1, keepdims=True)
        r = lax.rsqrt(ssq * (1.0 / (heads * HEAD_DIM)) + EPS)
        for hh in range(heads):
            cols = slice(hh * HEAD_DIM, (hh + 1) * HEAD_DIM)
            o_ref[:, cols] = ((acc_s[hh] * r) * g_ref[:, cols]).astype(o_ref.dtype)


def _mla_attention(q, k, v, g_out, *, seq, name):
    heads, t, _ = q.shape
    batch = t // seq
    tq = _tile(seq, 1024)
    tk = _tile(seq, 2 * LANES)
    nq = seq // tq
    width = heads * HEAD_DIM
    return pl.pallas_call(
        functools.partial(_attn_kernel, tk=tk),
        out_shape=jax.ShapeDtypeStruct((t, width), BF16),
        grid=(batch, nq, heads),
        in_specs=[pl.BlockSpec((1, tq, QK_PAD), lambda b, i, h: (h, b * nq + i, 0)),
                  pl.BlockSpec((1, seq, QK_PAD), lambda b, i, h: (h, b, 0)),
                  pl.BlockSpec((1, seq, QK_PAD), lambda b, i, h: (h, b, 0)),
                  pl.BlockSpec((1, width), lambda b, i, h: (0, 0))],
        out_specs=pl.BlockSpec((tq, width), lambda b, i, h: (b * nq + i, 0)),
        scratch_shapes=[pltpu.VMEM((heads, tq, HEAD_DIM), F32)],
        compiler_params=_params("parallel", "parallel", "arbitrary"),
        name=name,
    )(q, k, v, g_out.reshape(1, width))


def _xa_kv_kernel(mem_ref, g_ref, w_ref, k_ref, v_ref):
    m = _rms(mem_ref[...], g_ref[...]).astype(BF16)
    kv = jnp.dot(m, w_ref[...], preferred_element_type=F32)
    half = kv.shape[1] // 2
    k_ref[...] = kv[:, :half].astype(BF16)
    v_ref[...] = kv[:, half:].astype(BF16)


def _xa_kv(mem, g_mem, w_kv, *, name):
    rows, d = mem.shape
    xw = w_kv.shape[1] // 2
    tm = _tile(rows, 256)
    return pl.pallas_call(
        _xa_kv_kernel,
        out_shape=(jax.ShapeDtypeStruct((rows, xw), BF16), jax.ShapeDtypeStruct((rows, xw), BF16)),
        grid=(rows // tm,),
        in_specs=[pl.BlockSpec((tm, d), lambda i: (i, 0)),
                  pl.BlockSpec((1, d), lambda i: (0, 0)),
                  pl.BlockSpec((d, 2 * xw), lambda i: (0, 0))],
        out_specs=(pl.BlockSpec((tm, xw), lambda i: (i, 0)), pl.BlockSpec((tm, xw), lambda i: (i, 0))),
        compiler_params=_params("parallel"),
        name=name,
    )(mem, g_mem.reshape(1, d), w_kv)


def _xa_kernel(x_ref, gx_ref, wq_ref, k_ref, v_ref, wo_ref, gf_ref, x2_ref, h_ref, o_s, *, scale, parts):
    k = k_ref[0]
    v = v_ref[0]
    step = x_ref.shape[0] // parts
    for part in range(parts):
        rows = slice(part * step, (part + 1) * step)
        x = x_ref[rows, :]
        hx = _rms(x, gx_ref[...]).astype(BF16)
        q = (jnp.dot(hx, wq_ref[...], preferred_element_type=F32) * scale).astype(BF16)
        for hh in range(q.shape[1] // HEAD_DIM):
            cols = slice(hh * HEAD_DIM, (hh + 1) * HEAD_DIM)
            s = lax.dot_general(q[:, cols], k[:, cols], _NT_DIMS, preferred_element_type=F32)
            e = jnp.exp(s - jnp.max(s, axis=-1, keepdims=True))
            p = e * (1.0 / jnp.sum(e, axis=-1, keepdims=True))
            o_s[rows, cols] = jnp.dot(p.astype(BF16), v[:, cols], preferred_element_type=F32).astype(BF16)
        x2 = x + jnp.dot(o_s[rows, :], wo_ref[...], preferred_element_type=F32)
        x2_ref[rows, :] = x2
        h_ref[rows, :] = _rms(x2, gf_ref[...]).astype(h_ref.dtype)


def _cross_attention(x, g_xa, w_q, k_mem, v_mem, w_o, g_ffn, *, seq, name):
    t, d = x.shape
    _, n_mem, xw = k_mem.shape
    tm = _tile(seq, 512)
    parts = 2 if tm % (2 * LANES) == 0 else 1
    per_seq = seq // tm
    once = pl.Buffered(1)
    return pl.pallas_call(
        functools.partial(_xa_kernel, scale=HEAD_DIM ** -0.5, parts=parts),
        out_shape=(jax.ShapeDtypeStruct((t, d), F32), jax.ShapeDtypeStruct((t, d), BF16)),
        grid=(t // tm,),
        in_specs=[pl.BlockSpec((tm, d), lambda i: (i, 0)),
                  pl.BlockSpec((1, d), lambda i: (0, 0)),
                  pl.BlockSpec((d, xw), lambda i: (0, 0), pipeline_mode=once),
                  pl.BlockSpec((1, n_mem, xw), lambda i: (i // per_seq, 0, 0)),
                  pl.BlockSpec((1, n_mem, xw), lambda i: (i // per_seq, 0, 0)),
                  pl.BlockSpec((xw, d), lambda i: (0, 0), pipeline_mode=once),
                  pl.BlockSpec((1, d), lambda i: (0, 0))],
        out_specs=(pl.BlockSpec((tm, d), lambda i: (i, 0)), pl.BlockSpec((tm, d), lambda i: (i, 0))),
        scratch_shapes=[pltpu.VMEM((tm, xw), BF16)],
        compiler_params=_params("parallel"),
        name=name,
    )(x, g_xa.reshape(1, d), w_q, k_mem, v_mem, w_o, g_ffn.reshape(1, d))


def _rope_tables(seq):
    inv = 1.0 / (ROPE_THETA ** (jnp.arange(0, QK_ROPE, 2, dtype=F32) / QK_ROPE))
    ang = jnp.arange(seq, dtype=F32)[:, None] * inv[None, :]
    cos, sin = jnp.cos(ang), jnp.sin(ang)
    zeros = jnp.zeros((seq, LANES - QK_ROPE), F32)
    return (jnp.concatenate([cos, cos, zeros], axis=1),
            jnp.concatenate([-sin, sin, zeros], axis=1))


def _swap_halves(w):
    return jnp.concatenate([w[..., ROPE_HALF:], w[..., :ROPE_HALF]], axis=-1)


def _layout_w_in(w_in, gw, q_lora, kv_lora):
    w_c_q = w_in[:, 2 * gw:2 * gw + q_lora]
    w_c_kv = w_in[:, 2 * gw + q_lora:2 * gw + q_lora + kv_lora]
    w_kr = w_in[:, 2 * gw + q_lora + kv_lora:]
    latent = jnp.concatenate([w_c_kv, w_kr, _swap_halves(w_kr), w_c_q], axis=1)
    return w_in[:, :2 * gw].astype(BF16), latent.astype(BF16)


def _layout_w_uq(w_uq, heads):
    q_lora = w_uq.shape[0]
    w = w_uq.reshape(q_lora, heads, HEAD_DIM + QK_ROPE)
    rope = w[..., HEAD_DIM:]
    w = jnp.concatenate([w[..., :HEAD_DIM], rope, _swap_halves(rope)], axis=-1)
    return w.reshape(q_lora, heads * QK_PAD).astype(BF16)


def _ff_pad(d_ff):
    unit = FF_K_SPLIT * 2 * LANES
    return -(-d_ff // unit) * unit


def _cast_gate_up_kernel(g_ref, u_ref, o_ref):
    for p in range(g_ref.shape[1] // FF_GRANULE):
        src = slice(p * FF_GRANULE, (p + 1) * FF_GRANULE)
        o_ref[:, 2 * p * FF_GRANULE:(2 * p + 1) * FF_GRANULE] = g_ref[:, src].astype(BF16)
        o_ref[:, (2 * p + 1) * FF_GRANULE:(2 * p + 2) * FF_GRANULE] = u_ref[:, src].astype(BF16)


def _layout_gate_up(w_gate, w_up, *, layer):
    _, d, d_ff = w_gate.shape
    assert d_ff % FF_GRANULE == 0
    tr = _tile(d, LANES)
    spec = pl.BlockSpec((None, tr, d_ff), lambda r: (layer, r, 0))
    return pl.pallas_call(
        _cast_gate_up_kernel,
        out_shape=jax.ShapeDtypeStruct((d, 2 * d_ff), BF16),
        grid=(d // tr,),
        in_specs=[spec, spec],
        out_specs=pl.BlockSpec((tr, 2 * d_ff), lambda r: (r, 0)),
        compiler_params=_params("parallel"),
        name="cast_gate_up",
    )(w_gate, w_up)


def _cast_down_kernel(w_ref, o_ref, *, d_ff):
    row = lax.broadcasted_iota(jnp.int32, w_ref.shape, 0) + pl.program_id(0) * FF_BLOCK
    o_ref[...] = jnp.where(row < d_ff, w_ref[...], 0.0).astype(BF16)


def _layout_down(w_down, *, layer):
    _, d_ff, d = w_down.shape
    n_blocks = _ff_pad(d_ff) // FF_BLOCK
    assert n_blocks * FF_BLOCK - d_ff < FF_BLOCK
    tc = _tile(d, 2048)
    return pl.pallas_call(
        functools.partial(_cast_down_kernel, d_ff=d_ff),
        out_shape=jax.ShapeDtypeStruct((n_blocks * FF_BLOCK, d), BF16),
        grid=(n_blocks, d // tc),
        in_specs=[pl.BlockSpec((None, FF_BLOCK, tc), lambda r, c: (layer, r, c))],
        out_specs=pl.BlockSpec((FF_BLOCK, tc), lambda r, c: (r, c)),
        compiler_params=_params("parallel", "parallel"),
        name="cast_down",
    )(w_down)


def _ffn_up_kernel(x_ref, w_ref, *rest, n_real, n_side):
    cast_next = n_side is not None
    if cast_next:
        g_ref, u_ref, d_ref, o_ref, gu_ref, dn_ref = rest
        casting = pl.program_id(0) < n_side
    else:
        (o_ref,) = rest

    @pl.when(pl.program_id(1) < n_real)
    def _():
        half = x_ref.shape[0] // 2
        for rows in (slice(0, half), slice(half, 2 * half)):
            acc = jnp.dot(x_ref[rows, :], w_ref[...], preferred_element_type=F32)
            o_ref[rows, :] = (jax.nn.silu(acc[:, :FF_GRANULE]) * acc[:, FF_GRANULE:]).astype(o_ref.dtype)
        if cast_next:
            @pl.when(casting)
            def _():
                gu_ref[:, :FF_GRANULE] = g_ref[...].astype(BF16)
                gu_ref[:, FF_GRANULE:] = u_ref[...].astype(BF16)
                dn_ref[...] = d_ref[...].astype(BF16)

    @pl.when(pl.program_id(1) >= n_real)
    def _():
        o_ref[...] = jnp.zeros_like(o_ref)
        if cast_next:
            @pl.when(casting)
            def _():
                dn_ref[...] = jnp.zeros_like(dn_ref)


def _ffn_up(h, w_gu, *, next_weights, name):
    m, k = h.shape
    d_ff = w_gu.shape[1] // 2
    n_real = d_ff // FF_GRANULE
    n_blocks = _ff_pad(d_ff) // FF_GRANULE
    tm = _tile(m, 2048)
    last = n_real - 1
    n_side = None
    in_specs = [pl.BlockSpec((tm, k), lambda i, j: (i, 0)),
                pl.BlockSpec((k, 2 * FF_GRANULE), lambda i, j: (0, jnp.minimum(j, last)))]
    out_specs = [pl.BlockSpec((tm, FF_GRANULE), lambda i, j: (i, j))]
    out_shape = [jax.ShapeDtypeStruct((m, n_blocks * FF_GRANULE), BF16)]
    args = [h, w_gu]
    if next_weights is not None:
        w_gate, w_up, w_down, layer = next_weights
        cast_tile = 2 * FF_GRANULE
        n_side = k // cast_tile
        assert k % cast_tile == 0 and m // tm >= n_side

        def row_of(i):
            return jnp.minimum(i, n_side - 1)

        def col_of(i, j):
            return jnp.where(i < n_side, jnp.minimum(j, last), last)

        in_specs += [
            pl.BlockSpec((None, cast_tile, FF_GRANULE), lambda i, j: (layer, row_of(i), col_of(i, j))),
            pl.BlockSpec((None, cast_tile, FF_GRANULE), lambda i, j: (layer, row_of(i), col_of(i, j))),
            pl.BlockSpec((None, FF_GRANULE, cast_tile), lambda i, j: (layer, col_of(i, j), row_of(i)))]
        out_specs += [
            pl.BlockSpec((cast_tile, 2 * FF_GRANULE), lambda i, j: (row_of(i), col_of(i, j))),
            pl.BlockSpec((FF_GRANULE, cast_tile),
                         lambda i, j: (jnp.where(i < n_side, j, n_blocks - 1), row_of(i)))]
        out_shape += [jax.ShapeDtypeStruct((k, 2 * d_ff), BF16),
                      jax.ShapeDtypeStruct((n_blocks * FF_GRANULE, k), BF16)]
        args += [w_gate, w_up, w_down]
    out = pl.pallas_call(
        functools.partial(_ffn_up_kernel, n_real=n_real, n_side=n_side),
        out_shape=out_shape,
        grid=(m // tm, n_blocks),
        in_specs=in_specs,
        out_specs=out_specs,
        compiler_params=_params("arbitrary", "arbitrary"),
        name=name,
    )(*args)
    return out[0] if next_weights is None else tuple(out)


def kernel(x_prompt, x_sample, mem_prompt, mem_sample, norm_mix, w_in, sg_norm, sg_w, sg_b, mla_q_norm, mla_w_uq, mla_kv_norm, mla_w_ukv, out_norm_a, out_norm_b, w_out, norm_xa, norm_mem, xa_wq, xa_wk, xa_wv, xa_wo, norm_ffn, w_gate, w_up, w_down, norm_final):
    depth, d = norm_mix.shape
    seq = x_prompt.shape[1]
    assert x_sample.shape[1] == seq and mem_sample.shape[1] == mem_prompt.shape[1]
    gw = sg_norm.shape[1]
    chunk = sg_w.shape[2]
    q_lora, kv_lora = mla_q_norm.shape[1], mla_kv_norm.shape[1]
    heads = out_norm_b.shape[1] // HEAD_DIM
    n_mem = mem_prompt.shape[1]
    assert w_in.shape[2] == 2 * gw + q_lora + kv_lora + QK_ROPE
    assert mla_w_uq.shape[2] == heads * (HEAD_DIM + QK_ROPE) and mla_w_ukv.shape[2] == heads * QK_PAD

    rows_p = x_prompt.shape[0] * seq
    rows_s = x_sample.shape[0] * seq
    x = jnp.concatenate([x_prompt.reshape(rows_p, d), x_sample.reshape(rows_s, d)], axis=0)
    mem = jnp.concatenate([mem_prompt.reshape(-1, d), mem_sample.reshape(-1, d)], axis=0)
    batch = (rows_p + rows_s) // seq

    cos_t, sin_t = _rope_tables(seq)
    q_scale = (HEAD_DIM + QK_ROPE) ** -0.5 * math.log2(math.e)

    w_gu = _layout_gate_up(w_gate, w_up, layer=0)
    w_dn = _layout_down(w_down, layer=0)
    w_out_bf = w_out.astype(BF16)

    for l in range(depth):
        w_gated, w_latent = _layout_w_in(w_in[l], gw, q_lora, kv_lora)
        h_mix, zc = _norm_latent(x, norm_mix[l], w_latent, name=f"w_in_latent{l}")
        zg = _gelu_matmul(h_mix, w_gated, name=f"w_in_gated{l}")

        b_full = jnp.repeat(sg_b[l].T, HEAD_DIM, axis=1)
        ya = _spatial_gate(zg, sg_norm[l], sg_w[l].astype(BF16), b_full, out_norm_a[l], name=f"gate{l}")

        q, k, v = _mla_proj(zc, mla_q_norm[l], mla_kv_norm[l], _layout_w_uq(mla_w_uq[l], heads),
                            mla_w_ukv[l].astype(BF16), cos_t, sin_t, seq=seq, q_scale=q_scale,
                            name=f"mla_proj{l}")
        yb = _mla_attention(q, k, v, out_norm_b[l], seq=seq, name=f"mla_attn{l}")

        x = _mix_out(ya, yb, w_out_bf, x, layer=l, name=f"w_out{l}")

        w_xkv = jnp.concatenate([xa_wk[l], xa_wv[l]], axis=1).astype(BF16)
        k_mem, v_mem = _xa_kv(mem, norm_mem[l], w_xkv, name=f"xa_kv{l}")
        xw = k_mem.shape[1]
        x, h_ffn = _cross_attention(x, norm_xa[l], xa_wq[l].astype(BF16),
                                    k_mem.reshape(batch, n_mem, xw), v_mem.reshape(batch, n_mem, xw),
                                    xa_wo[l].astype(BF16), norm_ffn[l], seq=seq, name=f"xattn{l}")

        if l + 1 < depth:
            a, w_gu_next, w_dn_next = _ffn_up(h_ffn, w_gu, next_weights=(w_gate, w_up, w_down, l + 1),
                                              name=f"ffn_up{l}")
        else:
            a = _ffn_up(h_ffn, w_gu, next_weights=None, name=f"ffn_up{l}")
        x = _down_proj(a, w_dn, x, name=f"ffn_down{l}")
        if l + 1 < depth:
            w_gu, w_dn = w_gu_next, w_dn_next

    y_prompt = _rmsnorm(x, norm_final, F32, row_start=0, rows=rows_p, name="norm_final_prompt")
    y_sample = _rmsnorm(x, norm_final, F32, row_start=rows_p, rows=rows_s, name="norm_final_sample")
    return (y_prompt.reshape(x_prompt.shape), y_sample.reshape(x_sample.shape))
```

```python
import functools
import math

import jax
import jax.numpy as jnp
from jax import lax
from jax.experimental import pallas as pl
from jax.experimental.pallas import tpu as pltpu

F32 = jnp.float32
BF16 = jnp.bfloat16

EPS = 1e-6
ROPE_THETA = 10000.0
HEAD_DIM = 128
QK_ROPE = 64
ROPE_HALF = QK_ROPE // 2
QK_PAD = 2 * HEAD_DIM
LANES = 128
FF_GRANULE = 256
FF_BLOCK = 512
FF_K_SPLIT = 4
V7X_VMEM_LIMIT_BYTES = 60000 * 1024

_NT_DIMS = (((1,), (1,)), ((), ()))


def _params(*semantics):
    return pltpu.CompilerParams(dimension_semantics=semantics,
                                vmem_limit_bytes=V7X_VMEM_LIMIT_BYTES)


def _tile(n, pref):
    t = min(n, pref)
    while n % t:
        t //= 2
    return t


def _rms(x, g):
    r = lax.rsqrt(jnp.mean(x * x, axis=-1, keepdims=True) + EPS)
    return (x * r) * g


def _norm_kernel(x_ref, g_ref, o_ref):
    o_ref[...] = _rms(x_ref[...], g_ref[...]).astype(o_ref.dtype)


def _rmsnorm(x, g, out_dtype, *, row_start=0, rows=None, name):
    d = x.shape[1]
    rows = x.shape[0] if rows is None else rows
    tm = _tile(math.gcd(rows, row_start) if row_start else rows, 512)
    off = row_start // tm
    return pl.pallas_call(
        _norm_kernel,
        out_shape=jax.ShapeDtypeStruct((rows, d), out_dtype),
        grid=(rows // tm,),
        in_specs=[pl.BlockSpec((tm, d), lambda i: (i + off, 0)),
                  pl.BlockSpec((1, d), lambda i: (0, 0))],
        out_specs=pl.BlockSpec((tm, d), lambda i: (i, 0)),
        compiler_params=_params("parallel"),
        name=name,
    )(x, g.reshape(1, d))


def _norm_latent_kernel(x_ref, g_ref, w_ref, h_ref, z_ref, *, parts):
    step = x_ref.shape[0] // parts
    for part in range(parts):
        rows = slice(part * step, (part + 1) * step)
        h = _rms(x_ref[rows, :], g_ref[...]).astype(BF16)
        h_ref[rows, :] = h
        z_ref[rows, :] = jnp.dot(h, w_ref[...], preferred_element_type=F32)


def _norm_latent(x, g, w, *, layer, name):
    t, d = x.shape
    n = w.shape[2]
    tm = _tile(t, 512)
    parts = 2 if tm % (2 * LANES) == 0 else 1
    return pl.pallas_call(
        functools.partial(_norm_latent_kernel, parts=parts),
        out_shape=(jax.ShapeDtypeStruct((t, d), BF16), jax.ShapeDtypeStruct((t, n), F32)),
        grid=(t // tm,),
        in_specs=[pl.BlockSpec((tm, d), lambda i: (i, 0)),
                  pl.BlockSpec((1, d), lambda i: (0, 0)),
                  pl.BlockSpec((None, d, n), lambda i: (layer, 0, 0), pipeline_mode=pl.Buffered(1))],
        out_specs=(pl.BlockSpec((tm, d), lambda i: (i, 0)), pl.BlockSpec((tm, n), lambda i: (i, 0))),
        compiler_params=_params("parallel"),
        name=name,
    )(x, g.reshape(1, d), w)


def _gelu_mm_kernel(x_ref, w_ref, o_ref):
    acc = jnp.dot(x_ref[...], w_ref[...], preferred_element_type=F32)
    o_ref[...] = jax.nn.gelu(acc).astype(o_ref.dtype)


def _gelu_matmul(x, w, *, layer, name):
    m, k = x.shape
    n = w.shape[2]
    tm, tn = _tile(m, 1024), _tile(n, 1024)
    return pl.pallas_call(
        _gelu_mm_kernel,
        out_shape=jax.ShapeDtypeStruct((m, n), BF16),
        grid=(m // tm, n // tn),
        in_specs=[pl.BlockSpec((tm, k), lambda i, j: (i, 0)),
                  pl.BlockSpec((None, k, tn), lambda i, j: (layer, 0, j))],
        out_specs=pl.BlockSpec((tm, tn), lambda i, j: (i, j)),
        compiler_params=_params("parallel", "arbitrary"),
        name=name,
    )(x, w)


def _mix_out_kernel(ya_ref, yb_ref, wa_ref, wb_ref, x_ref, o_ref):
    acc = jnp.dot(ya_ref[...], wa_ref[...], preferred_element_type=F32)
    acc += jnp.dot(yb_ref[...], wb_ref[...], preferred_element_type=F32)
    o_ref[...] = x_ref[...] + acc


def _mix_out(ya, yb, w, x, *, layer, name):
    m, ka = ya.shape
    kb = yb.shape[1]
    assert ka == kb
    n = w.shape[2]
    tm, tn = _tile(m, 1024), _tile(n, 1024)
    return pl.pallas_call(
        _mix_out_kernel,
        out_shape=jax.ShapeDtypeStruct((m, n), F32),
        grid=(m // tm, n // tn),
        in_specs=[pl.BlockSpec((tm, ka), lambda i, j: (i, 0)),
                  pl.BlockSpec((tm, kb), lambda i, j: (i, 0)),
                  pl.BlockSpec((None, ka, tn), lambda i, j: (layer, 0, j)),
                  pl.BlockSpec((None, kb, tn), lambda i, j: (layer, 1, j)),
                  pl.BlockSpec((tm, tn), lambda i, j: (i, j))],
        out_specs=pl.BlockSpec((tm, tn), lambda i, j: (i, j)),
        compiler_params=_params("parallel", "arbitrary"),
        name=name,
    )(ya, yb, w, w, x)


def _down_kernel(a_ref, w_ref, x_ref, o_ref):
    @pl.when(pl.program_id(2) == 0)
    def _():
        o_ref[...] = x_ref[...] + jnp.dot(a_ref[...], w_ref[...], preferred_element_type=F32)

    @pl.when(pl.program_id(2) != 0)
    def _():
        o_ref[...] += jnp.dot(a_ref[...], w_ref[...], preferred_element_type=F32)


def _down_proj(a, w, x, *, name):
    m, k = a.shape
    n = w.shape[1]
    tm, tn = _tile(m, 1024), _tile(n, 1024)
    assert k % (FF_K_SPLIT * 2 * LANES) == 0
    nk = FF_K_SPLIT if k > 4096 else 1
    tk = k // nk
    return pl.pallas_call(
        _down_kernel,
        out_shape=jax.ShapeDtypeStruct((m, n), F32),
        grid=(m // tm, n // tn, nk),
        in_specs=[pl.BlockSpec((tm, tk), lambda i, j, kk: (i, kk)),
                  pl.BlockSpec((tk, tn), lambda i, j, kk: (kk, j)),
                  pl.BlockSpec((tm, tn), lambda i, j, kk: (i, j))],
        out_specs=pl.BlockSpec((tm, tn), lambda i, j, kk: (i, j)),
        compiler_params=_params("parallel", "arbitrary", "arbitrary"),
        name=name,
    )(a, w, x)


def _gate_kernel(u_ref, v_ref, gv_ref, ws_ref, b_ref, ga_ref, o_ref, mix_ref, *, n_chunks, chunk):
    groups = ws_ref.shape[0]
    vn = _rms(v_ref[...].astype(F32), gv_ref[...]).astype(BF16)
    for c in range(n_chunks):
        rows = slice(c * chunk, (c + 1) * chunk)
        for g in range(groups):
            cols = slice(g * HEAD_DIM, (g + 1) * HEAD_DIM)
            mix_ref[rows, cols] = jnp.dot(ws_ref[g], vn[rows, cols], preferred_element_type=F32)
        mix_ref[rows, :] += b_ref[...]
    ya = u_ref[...].astype(F32) * mix_ref[...]
    o_ref[...] = _rms(ya, ga_ref[...]).astype(o_ref.dtype)


def _spatial_gate(zg, g_v, w_s, b_full, g_a, *, layer, name):
    t, two_gw = zg.shape
    gw = two_gw // 2
    chunk = w_s.shape[2]
    tm = _tile(t, 2 * chunk)
    return pl.pallas_call(
        functools.partial(_gate_kernel, n_chunks=tm // chunk, chunk=chunk),
        out_shape=jax.ShapeDtypeStruct((t, gw), BF16),
        grid=(t // tm,),
        in_specs=[pl.BlockSpec((tm, gw), lambda i: (i, 0)),
                  pl.BlockSpec((tm, gw), lambda i: (i, 1)),
                  pl.BlockSpec((1, gw), lambda i: (0, 0)),
                  pl.BlockSpec((None,) + w_s.shape[1:], lambda i: (layer, 0, 0, 0)),
                  pl.BlockSpec((None, chunk, gw), lambda i: (layer, 0, 0)),
                  pl.BlockSpec((1, gw), lambda i: (0, 0))],
        out_specs=pl.BlockSpec((tm, gw), lambda i: (i, 0)),
        scratch_shapes=[pltpu.VMEM((tm, gw), F32)],
        compiler_params=_params("parallel"),
        name=name,
    )(zg, zg, g_v.reshape(1, gw), w_s, b_full, g_a.reshape(1, gw))


def _rope(slab, cos_t, sin_t):
    return slab * cos_t + pltpu.roll(slab, QK_ROPE, 1) * sin_t


def _mla_proj_kernel(zc_ref, gq_ref, gkv_ref, wq_ref, wkv_ref, cos_ref, sin_ref,
                     q_ref, k_ref, v_ref, hq_s, hkv_s, kpe_s, *, kv_lora, q_scale):
    heads = q_ref.shape[0]
    cos_t, sin_t = cos_ref[...], sin_ref[...]

    @pl.when(pl.program_id(1) == 0)
    def _():
        zc = zc_ref[...]
        hkv_s[...] = _rms(zc[:, :kv_lora], gkv_ref[...]).astype(BF16)
        kpe_s[...] = _rope(zc[:, kv_lora:kv_lora + LANES], cos_t, sin_t).astype(BF16)
        hq_s[...] = _rms(zc[:, kv_lora + LANES:], gq_ref[...]).astype(BF16)

    q_raw = jnp.dot(hq_s[...], wq_ref[...], preferred_element_type=F32)
    kv_raw = jnp.dot(hkv_s[...], wkv_ref[...], preferred_element_type=F32)
    lane = lax.broadcasted_iota(jnp.int32, (q_raw.shape[0], LANES), 1)
    ones_col = jnp.where(lane == 0, 1.0, 0.0).astype(BF16)
    for h in range(heads):
        base = h * QK_PAD
        q_ref[h, :, :HEAD_DIM] = (q_raw[:, base:base + HEAD_DIM] * q_scale).astype(BF16)
        q_pe = _rope(q_raw[:, base + HEAD_DIM:base + QK_PAD], cos_t, sin_t)
        q_ref[h, :, HEAD_DIM:] = (q_pe * q_scale).astype(BF16)
        k_ref[h, :, :HEAD_DIM] = kv_raw[:, base:base + HEAD_DIM].astype(BF16)
        k_ref[h, :, HEAD_DIM:] = kpe_s[...]
        v_ref[h, :, :HEAD_DIM] = kv_raw[:, base + HEAD_DIM:base + QK_PAD].astype(BF16)
        v_ref[h, :, HEAD_DIM:] = ones_col


def _mla_proj(zc, g_q, g_kv, w_q, w_kv, cos_t, sin_t, *, layer, seq, q_scale, name):
    t, zw = zc.shape
    kv_lora = g_kv.shape[0]
    q_lora = g_q.shape[0]
    heads = w_q.shape[2] // QK_PAD
    hb = _tile(heads, 4)
    tm = _tile(seq, 1024)
    n_pos = seq // tm
    return pl.pallas_call(
        functools.partial(_mla_proj_kernel, kv_lora=kv_lora, q_scale=q_scale),
        out_shape=(jax.ShapeDtypeStruct((heads, t, QK_PAD), BF16),
                   jax.ShapeDtypeStruct((heads, t, QK_PAD), BF16),
                   jax.ShapeDtypeStruct((heads, t, QK_PAD), BF16)),
        grid=(t // tm, heads // hb),
        in_specs=[pl.BlockSpec((tm, zw), lambda i, h: (i, 0)),
                  pl.BlockSpec((1, q_lora), lambda i, h: (0, 0)),
                  pl.BlockSpec((1, kv_lora), lambda i, h: (0, 0)),
                  pl.BlockSpec((None, q_lora, hb * QK_PAD), lambda i, h: (layer, 0, h)),
                  pl.BlockSpec((None, kv_lora, hb * QK_PAD), lambda i, h: (layer, 0, h)),
                  pl.BlockSpec((tm, LANES), lambda i, h: (i % n_pos, 0)),
                  pl.BlockSpec((tm, LANES), lambda i, h: (i % n_pos, 0))],
        out_specs=(pl.BlockSpec((hb, tm, QK_PAD), lambda i, h: (h, i, 0)),
                   pl.BlockSpec((hb, tm, QK_PAD), lambda i, h: (h, i, 0)),
                   pl.BlockSpec((hb, tm, QK_PAD), lambda i, h: (h, i, 0))),
        scratch_shapes=[pltpu.VMEM((tm, q_lora), BF16),
                        pltpu.VMEM((tm, kv_lora), BF16),
                        pltpu.VMEM((tm, LANES), BF16)],
        compiler_params=_params("parallel", "arbitrary"),
        name=name,
    )(zc, g_q.reshape(1, q_lora), g_kv.reshape(1, kv_lora), w_q, w_kv, cos_t, sin_t)


def _attn_kernel(q_ref, k_ref, v_ref, g_ref, o_ref, acc_s, *, tk):
    heads = acc_s.shape[0]
    h = pl.program_id(2)
    q = q_ref[0]
    tq = q.shape[0]
    m = jnp.full((tq, 1), -jnp.inf, F32)
    acc = jnp.zeros((tq, QK_PAD), F32)
    for c in range(k_ref.shape[1] // tk):
        rows = slice(c * tk, (c + 1) * tk)
        s = lax.dot_general(q, k_ref[0, rows, :], _NT_DIMS, preferred_element_type=F32)
        m_new = jnp.maximum(m, jnp.max(s, axis=-1, keepdims=True))
        p = jnp.exp2(s - m_new).astype(BF16)
        pv = jnp.dot(p, v_ref[0, rows, :], preferred_element_type=F32)
        acc = jnp.exp2(m - m_new) * acc + pv
        m = m_new
    acc_s[h] = acc[:, :HEAD_DIM] * (1.0 / acc[:, HEAD_DIM:HEAD_DIM + 1])

    @pl.when(h == heads - 1)
    def _():
        ssq = jnp.zeros((tq, 1), F32)
        for hh in range(heads):
            o = acc_s[hh]
            ssq += jnp.sum(o * o, axis=-1, keepdims=True)
        r = lax.rsqrt(ssq * (1.0 / (heads * HEAD_DIM)) + EPS)
        for hh in range(heads):
            cols = slice(hh * HEAD_DIM, (hh + 1) * HEAD_DIM)
            o_ref[:, cols] = ((acc_s[hh] * r) * g_ref[:, cols]).astype(o_ref.dtype)


def _mla_attention(q, k, v, g_out, *, seq, name):
    heads, t, _ = q.shape
    batch = t // seq
    tq = _tile(seq, 1024)
    tk = _tile(seq, 2 * LANES)
    nq = seq // tq
    width = heads * HEAD_DIM
    return pl.pallas_call(
        functools.partial(_attn_kernel, tk=tk),
        out_shape=jax.ShapeDtypeStruct((t, width), BF16),
        grid=(batch, nq, heads),
        in_specs=[pl.BlockSpec((1, tq, QK_PAD), lambda b, i, h: (h, b * nq + i, 0)),
                  pl.BlockSpec((1, seq, QK_PAD), lambda b, i, h: (h, b, 0)),
                  pl.BlockSpec((1, seq, QK_PAD), lambda b, i, h: (h, b, 0)),
                  pl.BlockSpec((1, width), lambda b, i, h: (0, 0))],
        out_specs=pl.BlockSpec((tq, width), lambda b, i, h: (b * nq + i, 0)),
        scratch_shapes=[pltpu.VMEM((heads, tq, HEAD_DIM), F32)],
        compiler_params=_params("parallel", "parallel", "arbitrary"),
        name=name,
    )(q, k, v, g_out.reshape(1, width))


def _xa_kv_kernel(mem_ref, g_ref, w_ref, k_ref, v_ref):
    m = _rms(mem_ref[...], g_ref[...]).astype(BF16)
    kv = jnp.dot(m, w_ref[...], preferred_element_type=F32)
    half = kv.shape[1] // 2
    k_ref[...] = kv[:, :half].astype(BF16)
    v_ref[...] = kv[:, half:].astype(BF16)


def _xa_kv(mem, g_mem, w_kv, *, layer, name):
    rows, d = mem.shape
    xw = w_kv.shape[2] // 2
    tm = _tile(rows, 256)
    return pl.pallas_call(
        _xa_kv_kernel,
        out_shape=(jax.ShapeDtypeStruct((rows, xw), BF16), jax.ShapeDtypeStruct((rows, xw), BF16)),
        grid=(rows // tm,),
        in_specs=[pl.BlockSpec((tm, d), lambda i: (i, 0)),
                  pl.BlockSpec((1, d), lambda i: (0, 0)),
                  pl.BlockSpec((None, d, 2 * xw), lambda i: (layer, 0, 0))],
        out_specs=(pl.BlockSpec((tm, xw), lambda i: (i, 0)), pl.BlockSpec((tm, xw), lambda i: (i, 0))),
        compiler_params=_params("parallel"),
        name=name,
    )(mem, g_mem.reshape(1, d), w_kv)


def _xa_kernel(x_ref, gx_ref, wq_ref, k_ref, v_ref, wo_ref, gf_ref, x2_ref, h_ref, o_s, *, scale, parts):
    k = k_ref[0]
    v = v_ref[0]
    step = x_ref.shape[0] // parts
    for part in range(parts):
        rows = slice(part * step, (part + 1) * step)
        x = x_ref[rows, :]
        hx = _rms(x, gx_ref[...]).astype(BF16)
        q = (jnp.dot(hx, wq_ref[...], preferred_element_type=F32) * scale).astype(BF16)
        for hh in range(q.shape[1] // HEAD_DIM):
            cols = slice(hh * HEAD_DIM, (hh + 1) * HEAD_DIM)
            s = lax.dot_general(q[:, cols], k[:, cols], _NT_DIMS, preferred_element_type=F32)
            e = jnp.exp(s - jnp.max(s, axis=-1, keepdims=True))
            p = e * (1.0 / jnp.sum(e, axis=-1, keepdims=True))
            o_s[rows, cols] = jnp.dot(p.astype(BF16), v[:, cols], preferred_element_type=F32).astype(BF16)
        x2 = x + jnp.dot(o_s[rows, :], wo_ref[...], preferred_element_type=F32)
        x2_ref[rows, :] = x2
        h_ref[rows, :] = _rms(x2, gf_ref[...]).astype(h_ref.dtype)


def _cross_attention(x, g_xa, w_q, k_mem, v_mem, w_o, g_ffn, *, layer, seq, name):
    t, d = x.shape
    _, n_mem, xw = k_mem.shape
    tm = _tile(seq, 512)
    parts = 2 if tm % (2 * LANES) == 0 else 1
    per_seq = seq // tm
    once = pl.Buffered(1)
    return pl.pallas_call(
        functools.partial(_xa_kernel, scale=HEAD_DIM ** -0.5, parts=parts),
        out_shape=(jax.ShapeDtypeStruct((t, d), F32), jax.ShapeDtypeStruct((t, d), BF16)),
        grid=(t // tm,),
        in_specs=[pl.BlockSpec((tm, d), lambda i: (i, 0)),
                  pl.BlockSpec((1, d), lambda i: (0, 0)),
                  pl.BlockSpec((None, d, xw), lambda i: (layer, 0, 0), pipeline_mode=once),
                  pl.BlockSpec((1, n_mem, xw), lambda i: (i // per_seq, 0, 0)),
                  pl.BlockSpec((1, n_mem, xw), lambda i: (i // per_seq, 0, 0)),
                  pl.BlockSpec((None, xw, d), lambda i: (layer, 0, 0), pipeline_mode=once),
                  pl.BlockSpec((1, d), lambda i: (0, 0))],
        out_specs=(pl.BlockSpec((tm, d), lambda i: (i, 0)), pl.BlockSpec((tm, d), lambda i: (i, 0))),
        scratch_shapes=[pltpu.VMEM((tm, xw), BF16)],
        compiler_params=_params("parallel"),
        name=name,
    )(x, g_xa.reshape(1, d), w_q, k_mem, v_mem, w_o, g_ffn.reshape(1, d))


def _rope_tables(seq):
    inv = 1.0 / (ROPE_THETA ** (jnp.arange(0, QK_ROPE, 2, dtype=F32) / QK_ROPE))
    ang = jnp.arange(seq, dtype=F32)[:, None] * inv[None, :]
    cos, sin = jnp.cos(ang), jnp.sin(ang)
    zeros = jnp.zeros((seq, LANES - QK_ROPE), F32)
    return (jnp.concatenate([cos, cos, zeros], axis=1),
            jnp.concatenate([-sin, sin, zeros], axis=1))


def _swap_halves(w):
    return jnp.concatenate([w[..., ROPE_HALF:], w[..., :ROPE_HALF]], axis=-1)


def _layout_w_in(w_in, gw, q_lora, kv_lora):
    w_c_q = w_in[..., 2 * gw:2 * gw + q_lora]
    w_c_kv = w_in[..., 2 * gw + q_lora:2 * gw + q_lora + kv_lora]
    w_kr = w_in[..., 2 * gw + q_lora + kv_lora:]
    latent = jnp.concatenate([w_c_kv, w_kr, _swap_halves(w_kr), w_c_q], axis=-1)
    return w_in[..., :2 * gw].astype(BF16), latent.astype(BF16)


def _layout_w_uq(w_uq, heads):
    lead = w_uq.shape[:-1]
    w = w_uq.reshape(lead + (heads, HEAD_DIM + QK_ROPE))
    rope = w[..., HEAD_DIM:]
    w = jnp.concatenate([w[..., :HEAD_DIM], rope, _swap_halves(rope)], axis=-1)
    return w.reshape(lead + (heads * QK_PAD,)).astype(BF16)


def _ff_pad(d_ff):
    unit = FF_K_SPLIT * 2 * LANES
    return -(-d_ff // unit) * unit


def _cast_gate_up_kernel(g_ref, u_ref, o_ref):
    for p in range(g_ref.shape[1] // FF_GRANULE):
        src = slice(p * FF_GRANULE, (p + 1) * FF_GRANULE)
        o_ref[:, 2 * p * FF_GRANULE:(2 * p + 1) * FF_GRANULE] = g_ref[:, src].astype(BF16)
        o_ref[:, (2 * p + 1) * FF_GRANULE:(2 * p + 2) * FF_GRANULE] = u_ref[:, src].astype(BF16)


def _layout_gate_up(w_gate, w_up, *, layer):
    _, d, d_ff = w_gate.shape
    assert d_ff % FF_GRANULE == 0
    tr = _tile(d, LANES)
    spec = pl.BlockSpec((None, tr, d_ff), lambda r: (layer, r, 0))
    return pl.pallas_call(
        _cast_gate_up_kernel,
        out_shape=jax.ShapeDtypeStruct((d, 2 * d_ff), BF16),
        grid=(d // tr,),
        in_specs=[spec, spec],
        out_specs=pl.BlockSpec((tr, 2 * d_ff), lambda r: (r, 0)),
        compiler_params=_params("parallel"),
        name="cast_gate_up",
    )(w_gate, w_up)


def _cast_down_kernel(w_ref, o_ref, *, d_ff):
    row = lax.broadcasted_iota(jnp.int32, w_ref.shape, 0) + pl.program_id(0) * FF_BLOCK
    o_ref[...] = jnp.where(row < d_ff, w_ref[...], 0.0).astype(BF16)


def _layout_down(w_down, *, layer):
    _, d_ff, d = w_down.shape
    n_blocks = _ff_pad(d_ff) // FF_BLOCK
    assert n_blocks * FF_BLOCK - d_ff < FF_BLOCK
    tc = _tile(d, 2048)
    return pl.pallas_call(
        functools.partial(_cast_down_kernel, d_ff=d_ff),
        out_shape=jax.ShapeDtypeStruct((n_blocks * FF_BLOCK, d), BF16),
        grid=(n_blocks, d // tc),
        in_specs=[pl.BlockSpec((None, FF_BLOCK, tc), lambda r, c: (layer, r, c))],
        out_specs=pl.BlockSpec((FF_BLOCK, tc), lambda r, c: (r, c)),
        compiler_params=_params("parallel", "parallel"),
        name="cast_down",
    )(w_down)


def _ffn_up_kernel(x_ref, w_ref, *rest, n_real, n_side):
    cast_next = n_side is not None
    if cast_next:
        g_ref, u_ref, d_ref, o_ref, gu_ref, dn_ref = rest
        casting = pl.program_id(0) < n_side
    else:
        (o_ref,) = rest

    @pl.when(pl.program_id(1) < n_real)
    def _():
        half = x_ref.shape[0] // 2
        for rows in (slice(0, half), slice(half, 2 * half)):
            acc = jnp.dot(x_ref[rows, :], w_ref[...], preferred_element_type=F32)
            o_ref[rows, :] = (jax.nn.silu(acc[:, :FF_GRANULE]) * acc[:, FF_GRANULE:]).astype(o_ref.dtype)
        if cast_next:
            @pl.when(casting)
            def _():
                gu_ref[:, :FF_GRANULE] = g_ref[...].astype(BF16)
                gu_ref[:, FF_GRANULE:] = u_ref[...].astype(BF16)
                dn_ref[...] = d_ref[...].astype(BF16)

    @pl.when(pl.program_id(1) >= n_real)
    def _():
        o_ref[...] = jnp.zeros_like(o_ref)
        if cast_next:
            @pl.when(casting)
            def _():
                dn_ref[...] = jnp.zeros_like(dn_ref)


def _ffn_up(h, w_gu, *, next_weights, name):
    m, k = h.shape
    d_ff = w_gu.shape[1] // 2
    n_real = d_ff // FF_GRANULE
    n_blocks = _ff_pad(d_ff) // FF_GRANULE
    tm = _tile(m, 2048)
    last = n_real - 1
    n_side = None
    in_specs = [pl.BlockSpec((tm, k), lambda i, j: (i, 0)),
                pl.BlockSpec((k, 2 * FF_GRANULE), lambda i, j: (0, jnp.minimum(j, last)))]
    out_specs = [pl.BlockSpec((tm, FF_GRANULE), lambda i, j: (i, j))]
    out_shape = [jax.ShapeDtypeStruct((m, n_blocks * FF_GRANULE), BF16)]
    args = [h, w_gu]
    if next_weights is not None:
        w_gate, w_up, w_down, layer = next_weights
        cast_tile = 2 * FF_GRANULE
        n_side = k // cast_tile
        assert k % cast_tile == 0 and m // tm >= n_side

        def row_of(i):
            return jnp.minimum(i, n_side - 1)

        def col_of(i, j):
            return jnp.where(i < n_side, jnp.minimum(j, last), last)

        in_specs += [
            pl.BlockSpec((None, cast_tile, FF_GRANULE), lambda i, j: (layer, row_of(i), col_of(i, j))),
            pl.BlockSpec((None, cast_tile, FF_GRANULE), lambda i, j: (layer, row_of(i), col_of(i, j))),
            pl.BlockSpec((None, FF_GRANULE, cast_tile), lambda i, j: (layer, col_of(i, j), row_of(i)))]
        out_specs += [
            pl.BlockSpec((cast_tile, 2 * FF_GRANULE), lambda i, j: (row_of(i), col_of(i, j))),
            pl.BlockSpec((FF_GRANULE, cast_tile),
                         lambda i, j: (jnp.where(i < n_side, j, n_blocks - 1), row_of(i)))]
        out_shape += [jax.ShapeDtypeStruct((k, 2 * d_ff), BF16),
                      jax.ShapeDtypeStruct((n_blocks * FF_GRANULE, k), BF16)]
        args += [w_gate, w_up, w_down]
    out = pl.pallas_call(
        functools.partial(_ffn_up_kernel, n_real=n_real, n_side=n_side),
        out_shape=out_shape,
        grid=(m // tm, n_blocks),
        in_specs=in_specs,
        out_specs=out_specs,
        compiler_params=_params("arbitrary", "arbitrary"),
        name=name,
    )(*args)
    return out[0] if next_weights is None else tuple(out)


def kernel(x_prompt, x_sample, mem_prompt, mem_sample, norm_mix, w_in, sg_norm, sg_w, sg_b, mla_q_norm, mla_w_uq, mla_kv_norm, mla_w_ukv, out_norm_a, out_norm_b, w_out, norm_xa, norm_mem, xa_wq, xa_wk, xa_wv, xa_wo, norm_ffn, w_gate, w_up, w_down, norm_final):
    depth, d = norm_mix.shape
    seq = x_prompt.shape[1]
    assert x_sample.shape[1] == seq and mem_sample.shape[1] == mem_prompt.shape[1]
    gw = sg_norm.shape[1]
    chunk = sg_w.shape[2]
    q_lora, kv_lora = mla_q_norm.shape[1], mla_kv_norm.shape[1]
    heads = out_norm_b.shape[1] // HEAD_DIM
    n_mem = mem_prompt.shape[1]
    assert w_in.shape[2] == 2 * gw + q_lora + kv_lora + QK_ROPE
    assert mla_w_uq.shape[2] == heads * (HEAD_DIM + QK_ROPE) and mla_w_ukv.shape[2] == heads * QK_PAD

    rows_p = x_prompt.shape[0] * seq
    rows_s = x_sample.shape[0] * seq
    x = jnp.concatenate([x_prompt.reshape(rows_p, d), x_sample.reshape(rows_s, d)], axis=0)
    mem = jnp.concatenate([mem_prompt.reshape(-1, d), mem_sample.reshape(-1, d)], axis=0)
    batch = (rows_p + rows_s) // seq

    cos_t, sin_t = _rope_tables(seq)
    q_scale = (HEAD_DIM + QK_ROPE) ** -0.5 * math.log2(math.e)

    w_gu = _layout_gate_up(w_gate, w_up, layer=0)
    w_dn = _layout_down(w_down, layer=0)
    w_out_bf = w_out.astype(BF16)
    w_gated, w_latent = _layout_w_in(w_in, gw, q_lora, kv_lora)
    w_uq_bf = _layout_w_uq(mla_w_uq, heads)
    w_ukv_bf = mla_w_ukv.astype(BF16)
    sg_w_bf = sg_w.astype(BF16)
    b_full = jnp.repeat(jnp.swapaxes(sg_b, 1, 2), HEAD_DIM, axis=2)
    w_xkv = jnp.concatenate([xa_wk, xa_wv], axis=2).astype(BF16)
    xa_wq_bf = xa_wq.astype(BF16)
    xa_wo_bf = xa_wo.astype(BF16)
    xw = xa_wq.shape[2]

    for l in range(depth):
        h_mix, zc = _norm_latent(x, norm_mix[l], w_latent, layer=l, name=f"w_in_latent{l}")
        zg = _gelu_matmul(h_mix, w_gated, layer=l, name=f"w_in_gated{l}")

        ya = _spatial_gate(zg, sg_norm[l], sg_w_bf, b_full, out_norm_a[l], layer=l, name=f"gate{l}")

        q, k, v = _mla_proj(zc, mla_q_norm[l], mla_kv_norm[l], w_uq_bf, w_ukv_bf, cos_t, sin_t,
                            layer=l, seq=seq, q_scale=q_scale, name=f"mla_proj{l}")
        yb = _mla_attention(q, k, v, out_norm_b[l], seq=seq, name=f"mla_attn{l}")

        x = _mix_out(ya, yb, w_out_bf, x, layer=l, name=f"w_out{l}")

        k_mem, v_mem = _xa_kv(mem, norm_mem[l], w_xkv, layer=l, name=f"xa_kv{l}")
        x, h_ffn = _cross_attention(x, norm_xa[l], xa_wq_bf,
                                    k_mem.reshape(batch, n_mem, xw), v_mem.reshape(batch, n_mem, xw),
                                    xa_wo_bf, norm_ffn[l], layer=l, seq=seq, name=f"xattn{l}")

        if l + 1 < depth:
            a, w_gu_next, w_dn_next = _ffn_up(h_ffn, w_gu, next_weights=(w_gate, w_up, w_down, l + 1),
                                              name=f"ffn_up{l}")
        else:
            a = _ffn_up(h_ffn, w_gu, next_weights=None, name=f"ffn_up{l}")
        x = _down_proj(a, w_dn, x, name=f"ffn_down{l}")
        if l + 1 < depth:
            w_gu, w_dn = w_gu_next, w_dn_next

    y_prompt = _rmsnorm(x, norm_final, F32, row_start=0, rows=rows_p, name="norm_final_prompt")
    y_sample = _rmsnorm(x, norm_final, F32, row_start=rows_p, rows=rows_s, name="norm_final_sample")
    return (y_prompt.reshape(x_prompt.shape), y_sample.reshape(x_sample.shape))
```

```python
import functools
import math

import jax
import jax.numpy as jnp
from jax import lax
from jax.experimental import pallas as pl
from jax.experimental.pallas import tpu as pltpu

F32 = jnp.float32
BF16 = jnp.bfloat16

EPS = 1e-6
ROPE_THETA = 10000.0
HEAD_DIM = 128
QK_ROPE = 64
ROPE_HALF = QK_ROPE // 2
QK_PAD = 2 * HEAD_DIM
LANES = 128
FF_GRANULE = 256
FF_BLOCK = 512
FF_K_SPLIT = 4
V7X_VMEM_LIMIT_BYTES = 60000 * 1024

_NT_DIMS = (((1,), (1,)), ((), ()))


def _params(*semantics):
    return pltpu.CompilerParams(dimension_semantics=semantics,
                                vmem_limit_bytes=V7X_VMEM_LIMIT_BYTES)


def _tile(n, pref):
    t = min(n, pref)
    while n % t:
        t //= 2
    return t


def _rms(x, g):
    r = lax.rsqrt(jnp.mean(x * x, axis=-1, keepdims=True) + EPS)
    return (x * r) * g


def _norm_kernel(x_ref, g_ref, o_ref):
    o_ref[...] = _rms(x_ref[...], g_ref[...]).astype(o_ref.dtype)


def _rmsnorm(x, g, out_dtype, *, row_start=0, rows=None, name):
    d = x.shape[1]
    rows = x.shape[0] if rows is None else rows
    tm = _tile(math.gcd(rows, row_start) if row_start else rows, 512)
    off = row_start // tm
    return pl.pallas_call(
        _norm_kernel,
        out_shape=jax.ShapeDtypeStruct((rows, d), out_dtype),
        grid=(rows // tm,),
        in_specs=[pl.BlockSpec((tm, d), lambda i: (i + off, 0)),
                  pl.BlockSpec((1, d), lambda i: (0, 0))],
        out_specs=pl.BlockSpec((tm, d), lambda i: (i, 0)),
        compiler_params=_params("parallel"),
        name=name,
    )(x, g.reshape(1, d))


def _norm_latent_kernel(x_ref, g_ref, w_ref, h_ref, z_ref, *, parts):
    step = x_ref.shape[0] // parts
    for part in range(parts):
        rows = slice(part * step, (part + 1) * step)
        h = _rms(x_ref[rows, :], g_ref[...]).astype(BF16)
        h_ref[rows, :] = h
        z_ref[rows, :] = jnp.dot(h, w_ref[...], preferred_element_type=F32)


def _norm_latent(x, g, w, *, name):
    t, d = x.shape
    n = w.shape[1]
    tm = _tile(t, 512)
    parts = 2 if tm % (2 * LANES) == 0 else 1
    return pl.pallas_call(
        functools.partial(_norm_latent_kernel, parts=parts),
        out_shape=(jax.ShapeDtypeStruct((t, d), BF16), jax.ShapeDtypeStruct((t, n), F32)),
        grid=(t // tm,),
        in_specs=[pl.BlockSpec((tm, d), lambda i: (i, 0)),
                  pl.BlockSpec((1, d), lambda i: (0, 0)),
                  pl.BlockSpec((d, n), lambda i: (0, 0), pipeline_mode=pl.Buffered(1))],
        out_specs=(pl.BlockSpec((tm, d), lambda i: (i, 0)), pl.BlockSpec((tm, n), lambda i: (i, 0))),
        compiler_params=_params("parallel"),
        name=name,
    )(x, g.reshape(1, d), w)


def _gelu_mm_kernel(x_ref, w_ref, o_ref):
    acc = jnp.dot(x_ref[...], w_ref[...], preferred_element_type=F32)
    o_ref[...] = jax.nn.gelu(acc).astype(o_ref.dtype)


def _gelu_matmul(x, w, *, name):
    m, k = x.shape
    n = w.shape[1]
    tm, tn = _tile(m, 1024), _tile(n, 1024)
    return pl.pallas_call(
        _gelu_mm_kernel,
        out_shape=jax.ShapeDtypeStruct((m, n), BF16),
        grid=(m // tm, n // tn),
        in_specs=[pl.BlockSpec((tm, k), lambda i, j: (i, 0)),
                  pl.BlockSpec((k, tn), lambda i, j: (0, j))],
        out_specs=pl.BlockSpec((tm, tn), lambda i, j: (i, j)),
        compiler_params=_params("parallel", "arbitrary"),
        name=name,
    )(x, w)


def _mix_out_kernel(ya_ref, yb_ref, wa_ref, wb_ref, x_ref, o_ref):
    acc = jnp.dot(ya_ref[...], wa_ref[...], preferred_element_type=F32)
    acc += jnp.dot(yb_ref[...], wb_ref[...], preferred_element_type=F32)
    o_ref[...] = x_ref[...] + acc


def _mix_out(ya, yb, w, x, *, layer, name):
    m, ka = ya.shape
    kb = yb.shape[1]
    assert ka == kb
    n = w.shape[2]
    tm, tn = _tile(m, 1024), _tile(n, 1024)
    return pl.pallas_call(
        _mix_out_kernel,
        out_shape=jax.ShapeDtypeStruct((m, n), F32),
        grid=(m // tm, n // tn),
        in_specs=[pl.BlockSpec((tm, ka), lambda i, j: (i, 0)),
                  pl.BlockSpec((tm, kb), lambda i, j: (i, 0)),
                  pl.BlockSpec((None, ka, tn), lambda i, j: (layer, 0, j)),
                  pl.BlockSpec((None, kb, tn), lambda i, j: (layer, 1, j)),
                  pl.BlockSpec((tm, tn), lambda i, j: (i, j))],
        out_specs=pl.BlockSpec((tm, tn), lambda i, j: (i, j)),
        compiler_params=_params("parallel", "arbitrary"),
        name=name,
    )(ya, yb, w, w, x)


def _down_kernel(a_ref, w_ref, x_ref, o_ref):
    @pl.when(pl.program_id(2) == 0)
    def _():
        o_ref[...] = x_ref[...] + jnp.dot(a_ref[...], w_ref[...], preferred_element_type=F32)

    @pl.when(pl.program_id(2) != 0)
    def _():
        o_ref[...] += jnp.dot(a_ref[...], w_ref[...], preferred_element_type=F32)


def _down_proj(a, w, x, *, name):
    m, k = a.shape
    n = w.shape[1]
    tm, tn = _tile(m, 1024), _tile(n, 1024)
    assert k % (FF_K_SPLIT * 2 * LANES) == 0
    nk = FF_K_SPLIT if k > 4096 else 1
    tk = k // nk
    return pl.pallas_call(
        _down_kernel,
        out_shape=jax.ShapeDtypeStruct((m, n), F32),
        grid=(m // tm, n // tn, nk),
        in_specs=[pl.BlockSpec((tm, tk), lambda i, j, kk: (i, kk)),
                  pl.BlockSpec((tk, tn), lambda i, j, kk: (kk, j)),
                  pl.BlockSpec((tm, tn), lambda i, j, kk: (i, j))],
        out_specs=pl.BlockSpec((tm, tn), lambda i, j, kk: (i, j)),
        compiler_params=_params("parallel", "arbitrary", "arbitrary"),
        name=name,
    )(a, w, x)


def _gate_kernel(u_ref, v_ref, gv_ref, ws_ref, b_ref, ga_ref, o_ref, mix_ref, *, n_chunks, chunk):
    groups = ws_ref.shape[0]
    vn = _rms(v_ref[...].astype(F32), gv_ref[...]).astype(BF16)
    for c in range(n_chunks):
        rows = slice(c * chunk, (c + 1) * chunk)
        for g in range(groups):
            cols = slice(g * HEAD_DIM, (g + 1) * HEAD_DIM)
            mix_ref[rows, cols] = jnp.dot(ws_ref[g], vn[rows, cols], preferred_element_type=F32)
        mix_ref[rows, :] += b_ref[...]
    ya = u_ref[...].astype(F32) * mix_ref[...]
    o_ref[...] = _rms(ya, ga_ref[...]).astype(o_ref.dtype)


def _spatial_gate(zg, g_v, w_s, b_full, g_a, *, layer, name):
    t, two_gw = zg.shape
    gw = two_gw // 2
    chunk = w_s.shape[2]
    tm = _tile(t, 2 * chunk)
    return pl.pallas_call(
        functools.partial(_gate_kernel, n_chunks=tm // chunk, chunk=chunk),
        out_shape=jax.ShapeDtypeStruct((t, gw), BF16),
        grid=(t // tm,),
        in_specs=[pl.BlockSpec((tm, gw), lambda i: (i, 0)),
                  pl.BlockSpec((tm, gw), lambda i: (i, 1)),
                  pl.BlockSpec((1, gw), lambda i: (0, 0)),
                  pl.BlockSpec((None,) + w_s.shape[1:], lambda i: (layer, 0, 0, 0)),
                  pl.BlockSpec((None, chunk, gw), lambda i: (layer, 0, 0)),
                  pl.BlockSpec((1, gw), lambda i: (0, 0))],
        out_specs=pl.BlockSpec((tm, gw), lambda i: (i, 0)),
        scratch_shapes=[pltpu.VMEM((tm, gw), F32)],
        compiler_params=_params("parallel"),
        name=name,
    )(zg, zg, g_v.reshape(1, gw), w_s, b_full, g_a.reshape(1, gw))


def _rope(slab, cos_t, sin_t):
    return slab * cos_t + pltpu.roll(slab, QK_ROPE, 1) * sin_t


def _mla_proj_kernel(zc_ref, gq_ref, gkv_ref, wq_ref, wkv_ref, cos_ref, sin_ref,
                     q_ref, k_ref, v_ref, hq_s, hkv_s, kpe_s, *, kv_lora, q_scale):
    heads = q_ref.shape[0]
    cos_t, sin_t = cos_ref[...], sin_ref[...]

    @pl.when(pl.program_id(1) == 0)
    def _():
        zc = zc_ref[...]
        hkv_s[...] = _rms(zc[:, :kv_lora], gkv_ref[...]).astype(BF16)
        kpe_s[...] = _rope(zc[:, kv_lora:kv_lora + LANES], cos_t, sin_t).astype(BF16)
        hq_s[...] = _rms(zc[:, kv_lora + LANES:], gq_ref[...]).astype(BF16)

    q_raw = jnp.dot(hq_s[...], wq_ref[...], preferred_element_type=F32)
    kv_raw = jnp.dot(hkv_s[...], wkv_ref[...], preferred_element_type=F32)
    lane = lax.broadcasted_iota(jnp.int32, (q_raw.shape[0], LANES), 1)
    ones_col = jnp.where(lane == 0, 1.0, 0.0).astype(BF16)
    for h in range(heads):
        base = h * QK_PAD
        q_ref[h, :, :HEAD_DIM] = (q_raw[:, base:base + HEAD_DIM] * q_scale).astype(BF16)
        q_pe = _rope(q_raw[:, base + HEAD_DIM:base + QK_PAD], cos_t, sin_t)
        q_ref[h, :, HEAD_DIM:] = (q_pe * q_scale).astype(BF16)
        k_ref[h, :, :HEAD_DIM] = kv_raw[:, base:base + HEAD_DIM].astype(BF16)
        k_ref[h, :, HEAD_DIM:] = kpe_s[...]
        v_ref[h, :, :HEAD_DIM] = kv_raw[:, base + HEAD_DIM:base + QK_PAD].astype(BF16)
        v_ref[h, :, HEAD_DIM:] = ones_col


def _mla_proj(zc, g_q, g_kv, w_q, w_kv, cos_t, sin_t, *, layer, seq, q_scale, name):
    t, zw = zc.shape
    kv_lora = g_kv.shape[0]
    q_lora = g_q.shape[0]
    heads = w_q.shape[2] // QK_PAD
    hb = _tile(heads, 4)
    tm = _tile(seq, 1024)
    n_pos = seq // tm
    return pl.pallas_call(
        functools.partial(_mla_proj_kernel, kv_lora=kv_lora, q_scale=q_scale),
        out_shape=(jax.ShapeDtypeStruct((heads, t, QK_PAD), BF16),
                   jax.ShapeDtypeStruct((heads, t, QK_PAD), BF16),
                   jax.ShapeDtypeStruct((heads, t, QK_PAD), BF16)),
        grid=(t // tm, heads // hb),
        in_specs=[pl.BlockSpec((tm, zw), lambda i, h: (i, 0)),
                  pl.BlockSpec((1, q_lora), lambda i, h: (0, 0)),
                  pl.BlockSpec((1, kv_lora), lambda i, h: (0, 0)),
                  pl.BlockSpec((None, q_lora, hb * QK_PAD), lambda i, h: (layer, 0, h)),
                  pl.BlockSpec((None, kv_lora, hb * QK_PAD), lambda i, h: (layer, 0, h)),
                  pl.BlockSpec((tm, LANES), lambda i, h: (i % n_pos, 0)),
                  pl.BlockSpec((tm, LANES), lambda i, h: (i % n_pos, 0))],
        out_specs=(pl.BlockSpec((hb, tm, QK_PAD), lambda i, h: (h, i, 0)),
                   pl.BlockSpec((hb, tm, QK_PAD), lambda i, h: (h, i, 0)),
                   pl.BlockSpec((hb, tm, QK_PAD), lambda i, h: (h, i, 0))),
        scratch_shapes=[pltpu.VMEM((tm, q_lora), BF16),
                        pltpu.VMEM((tm, kv_lora), BF16),
                        pltpu.VMEM((tm, LANES), BF16)],
        compiler_params=_params("parallel", "arbitrary"),
        name=name,
    )(zc, g_q.reshape(1, q_lora), g_kv.reshape(1, kv_lora), w_q, w_kv, cos_t, sin_t)


def _attn_kernel(q_ref, k_ref, v_ref, g_ref, o_ref, acc_s, *, tk):
    heads = acc_s.shape[0]
    h = pl.program_id(2)
    q = q_ref[0]
    tq = q.shape[0]
    m = jnp.full((tq, 1), -jnp.inf, F32)
    acc = jnp.zeros((tq, QK_PAD), F32)
    for c in range(k_ref.shape[1] // tk):
        rows = slice(c * tk, (c + 1) * tk)
        s = lax.dot_general(q, k_ref[0, rows, :], _NT_DIMS, preferred_element_type=F32)
        m_new = jnp.maximum(m, jnp.max(s, axis=-1, keepdims=True))
        p = jnp.exp2(s - m_new).astype(BF16)
        pv = jnp.dot(p, v_ref[0, rows, :], preferred_element_type=F32)
        acc = jnp.exp2(m - m_new) * acc + pv
        m = m_new
    acc_s[h] = acc[:, :HEAD_DIM] * (1.0 / acc[:, HEAD_DIM:HEAD_DIM + 1])

    @pl.when(h == heads - 1)
    def _():
        ssq = jnp.zeros((tq, 1), F32)
        for hh in range(heads):
            o = acc_s[hh]
            ssq += jnp.sum(o * o, axis=-1, keepdims=True)
        r = lax.rsqrt(ssq * (1.0 / (heads * HEAD_DIM)) + EPS)
        for hh in range(heads):
            cols = slice(hh * HEAD_DIM, (hh + 1) * HEAD_DIM)
            o_ref[:, cols] = ((acc_s[hh] * r) * g_ref[:, cols]).astype(o_ref.dtype)


def _mla_attention(q, k, v, g_out, *, seq, name):
    heads, t, _ = q.shape
    batch = t // seq
    tq = _tile(seq, 1024)
    tk = _tile(seq, 2 * LANES)
    nq = seq // tq
    width = heads * HEAD_DIM
    return pl.pallas_call(
        functools.partial(_attn_kernel, tk=tk),
        out_shape=jax.ShapeDtypeStruct((t, width), BF16),
        grid=(batch, nq, heads),
        in_specs=[pl.BlockSpec((1, tq, QK_PAD), lambda b, i, h: (h, b * nq + i, 0)),
                  pl.BlockSpec((1, seq, QK_PAD), lambda b, i, h: (h, b, 0)),
                  pl.BlockSpec((1, seq, QK_PAD), lambda b, i, h: (h, b, 0)),
                  pl.BlockSpec((1, width), lambda b, i, h: (0, 0))],
        out_specs=pl.BlockSpec((tq, width), lambda b, i, h: (b * nq + i, 0)),
        scratch_shapes=[pltpu.VMEM((heads, tq, HEAD_DIM), F32)],
        compiler_params=_params("parallel", "parallel", "arbitrary"),
        name=name,
    )(q, k, v, g_out.reshape(1, width))


def _xa_kv_kernel(mem_ref, g_ref, w_ref, k_ref, v_ref):
    m = _rms(mem_ref[...], g_ref[...]).astype(BF16)
    kv = jnp.dot(m, w_ref[...], preferred_element_type=F32)
    half = kv.shape[1] // 2
    k_ref[...] = kv[:, :half].astype(BF16)
    v_ref[...] = kv[:, half:].astype(BF16)


def _xa_kv(mem, g_mem, w_kv, *, layer, name):
    rows, d = mem.shape
    xw = w_kv.shape[2] // 2
    tm = _tile(rows, 256)
    return pl.pallas_call(
        _xa_kv_kernel,
        out_shape=(jax.ShapeDtypeStruct((rows, xw), BF16), jax.ShapeDtypeStruct((rows, xw), BF16)),
        grid=(rows // tm,),
        in_specs=[pl.BlockSpec((tm, d), lambda i: (i, 0)),
                  pl.BlockSpec((1, d), lambda i: (0, 0)),
                  pl.BlockSpec((None, d, 2 * xw), lambda i: (layer, 0, 0))],
        out_specs=(pl.BlockSpec((tm, xw), lambda i: (i, 0)), pl.BlockSpec((tm, xw), lambda i: (i, 0))),
        compiler_params=_params("parallel"),
        name=name,
    )(mem, g_mem.reshape(1, d), w_kv)


def _xa_kernel(x_ref, gx_ref, wq_ref, k_ref, v_ref, wo_ref, gf_ref, x2_ref, h_ref, o_s, *, scale, parts):
    k = k_ref[0]
    v = v_ref[0]
    step = x_ref.shape[0] // parts
    for part in range(parts):
        rows = slice(part * step, (part + 1) * step)
        x = x_ref[rows, :]
        hx = _rms(x, gx_ref[...]).astype(BF16)
        q = (jnp.dot(hx, wq_ref[...], preferred_element_type=F32) * scale).astype(BF16)
        for hh in range(q.shape[1] // HEAD_DIM):
            cols = slice(hh * HEAD_DIM, (hh + 1) * HEAD_DIM)
            s = lax.dot_general(q[:, cols], k[:, cols], _NT_DIMS, preferred_element_type=F32)
            e = jnp.exp(s - jnp.max(s, axis=-1, keepdims=True))
            p = e * (1.0 / jnp.sum(e, axis=-1, keepdims=True))
            o_s[rows, cols] = jnp.dot(p.astype(BF16), v[:, cols], preferred_element_type=F32).astype(BF16)
        x2 = x + jnp.dot(o_s[rows, :], wo_ref[...], preferred_element_type=F32)
        x2_ref[rows, :] = x2
        h_ref[rows, :] = _rms(x2, gf_ref[...]).astype(h_ref.dtype)


def _cross_attention(x, g_xa, w_q, k_mem, v_mem, w_o, g_ffn, *, layer, seq, name):
    t, d = x.shape
    _, n_mem, xw = k_mem.shape
    tm = _tile(seq, 512)
    parts = 2 if tm % (2 * LANES) == 0 else 1
    per_seq = seq // tm
    once = pl.Buffered(1)
    return pl.pallas_call(
        functools.partial(_xa_kernel, scale=HEAD_DIM ** -0.5, parts=parts),
        out_shape=(jax.ShapeDtypeStruct((t, d), F32), jax.ShapeDtypeStruct((t, d), BF16)),
        grid=(t // tm,),
        in_specs=[pl.BlockSpec((tm, d), lambda i: (i, 0)),
                  pl.BlockSpec((1, d), lambda i: (0, 0)),
                  pl.BlockSpec((None, d, xw), lambda i: (layer, 0, 0), pipeline_mode=once),
                  pl.BlockSpec((1, n_mem, xw), lambda i: (i // per_seq, 0, 0)),
                  pl.BlockSpec((1, n_mem, xw), lambda i: (i // per_seq, 0, 0)),
                  pl.BlockSpec((None, xw, d), lambda i: (layer, 0, 0), pipeline_mode=once),
                  pl.BlockSpec((1, d), lambda i: (0, 0))],
        out_specs=(pl.BlockSpec((tm, d), lambda i: (i, 0)), pl.BlockSpec((tm, d), lambda i: (i, 0))),
        scratch_shapes=[pltpu.VMEM((tm, xw), BF16)],
        compiler_params=_params("parallel"),
        name=name,
    )(x, g_xa.reshape(1, d), w_q, k_mem, v_mem, w_o, g_ffn.reshape(1, d))


def _rope_tables(seq):
    inv = 1.0 / (ROPE_THETA ** (jnp.arange(0, QK_ROPE, 2, dtype=F32) / QK_ROPE))
    ang = jnp.arange(seq, dtype=F32)[:, None] * inv[None, :]
    cos, sin = jnp.cos(ang), jnp.sin(ang)
    zeros = jnp.zeros((seq, LANES - QK_ROPE), F32)
    return (jnp.concatenate([cos, cos, zeros], axis=1),
            jnp.concatenate([-sin, sin, zeros], axis=1))


def _swap_halves(w):
    return jnp.concatenate([w[..., ROPE_HALF:], w[..., :ROPE_HALF]], axis=-1)


def _layout_w_in(w_in, gw, q_lora, kv_lora):
    w_c_q = w_in[..., 2 * gw:2 * gw + q_lora]
    w_c_kv = w_in[..., 2 * gw + q_lora:2 * gw + q_lora + kv_lora]
    w_kr = w_in[..., 2 * gw + q_lora + kv_lora:]
    latent = jnp.concatenate([w_c_kv, w_kr, _swap_halves(w_kr), w_c_q], axis=-1)
    return w_in[..., :2 * gw].astype(BF16), latent.astype(BF16)


def _layout_w_uq(w_uq, heads):
    lead = w_uq.shape[:-1]
    w = w_uq.reshape(lead + (heads, HEAD_DIM + QK_ROPE))
    rope = w[..., HEAD_DIM:]
    w = jnp.concatenate([w[..., :HEAD_DIM], rope, _swap_halves(rope)], axis=-1)
    return w.reshape(lead + (heads * QK_PAD,)).astype(BF16)


def _ff_pad(d_ff):
    unit = FF_K_SPLIT * 2 * LANES
    return -(-d_ff // unit) * unit


def _cast_gate_up_kernel(g_ref, u_ref, o_ref):
    for p in range(g_ref.shape[1] // FF_GRANULE):
        src = slice(p * FF_GRANULE, (p + 1) * FF_GRANULE)
        o_ref[:, 2 * p * FF_GRANULE:(2 * p + 1) * FF_GRANULE] = g_ref[:, src].astype(BF16)
        o_ref[:, (2 * p + 1) * FF_GRANULE:(2 * p + 2) * FF_GRANULE] = u_ref[:, src].astype(BF16)


def _layout_gate_up(w_gate, w_up, *, layer):
    _, d, d_ff = w_gate.shape
    assert d_ff % FF_GRANULE == 0
    tr = _tile(d, LANES)
    spec = pl.BlockSpec((None, tr, d_ff), lambda r: (layer, r, 0))
    return pl.pallas_call(
        _cast_gate_up_kernel,
        out_shape=jax.ShapeDtypeStruct((d, 2 * d_ff), BF16),
        grid=(d // tr,),
        in_specs=[spec, spec],
        out_specs=pl.BlockSpec((tr, 2 * d_ff), lambda r: (r, 0)),
        compiler_params=_params("parallel"),
        name="cast_gate_up",
    )(w_gate, w_up)


def _cast_down_kernel(w_ref, o_ref, *, d_ff):
    row = lax.broadcasted_iota(jnp.int32, w_ref.shape, 0) + pl.program_id(0) * FF_BLOCK
    o_ref[...] = jnp.where(row < d_ff, w_ref[...], 0.0).astype(BF16)


def _layout_down(w_down, *, layer):
    _, d_ff, d = w_down.shape
    n_blocks = _ff_pad(d_ff) // FF_BLOCK
    assert n_blocks * FF_BLOCK - d_ff < FF_BLOCK
    tc = _tile(d, 2048)
    return pl.pallas_call(
        functools.partial(_cast_down_kernel, d_ff=d_ff),
        out_shape=jax.ShapeDtypeStruct((n_blocks * FF_BLOCK, d), BF16),
        grid=(n_blocks, d // tc),
        in_specs=[pl.BlockSpec((None, FF_BLOCK, tc), lambda r, c: (layer, r, c))],
        out_specs=pl.BlockSpec((FF_BLOCK, tc), lambda r, c: (r, c)),
        compiler_params=_params("parallel", "parallel"),
        name="cast_down",
    )(w_down)


def _ffn_up_kernel(x_ref, w_ref, *rest, n_real, n_side):
    cast_next = n_side is not None
    if cast_next:
        g_ref, u_ref, d_ref, o_ref, gu_ref, dn_ref = rest
        casting = pl.program_id(0) < n_side
    else:
        (o_ref,) = rest

    @pl.when(pl.program_id(1) < n_real)
    def _():
        half = x_ref.shape[0] // 2
        for rows in (slice(0, half), slice(half, 2 * half)):
            acc = jnp.dot(x_ref[rows, :], w_ref[...], preferred_element_type=F32)
            o_ref[rows, :] = (jax.nn.silu(acc[:, :FF_GRANULE]) * acc[:, FF_GRANULE:]).astype(o_ref.dtype)
        if cast_next:
            @pl.when(casting)
            def _():
                gu_ref[:, :FF_GRANULE] = g_ref[...].astype(BF16)
                gu_ref[:, FF_GRANULE:] = u_ref[...].astype(BF16)
                dn_ref[...] = d_ref[...].astype(BF16)

    @pl.when(pl.program_id(1) >= n_real)
    def _():
        o_ref[...] = jnp.zeros_like(o_ref)
        if cast_next:
            @pl.when(casting)
            def _():
                dn_ref[...] = jnp.zeros_like(dn_ref)


def _ffn_up(h, w_gu, *, next_weights, name):
    m, k = h.shape
    d_ff = w_gu.shape[1] // 2
    n_real = d_ff // FF_GRANULE
    n_blocks = _ff_pad(d_ff) // FF_GRANULE
    tm = _tile(m, 2048)
    last = n_real - 1
    n_side = None
    in_specs = [pl.BlockSpec((tm, k), lambda i, j: (i, 0)),
                pl.BlockSpec((k, 2 * FF_GRANULE), lambda i, j: (0, jnp.minimum(j, last)))]
    out_specs = [pl.BlockSpec((tm, FF_GRANULE), lambda i, j: (i, j))]
    out_shape = [jax.ShapeDtypeStruct((m, n_blocks * FF_GRANULE), BF16)]
    args = [h, w_gu]
    if next_weights is not None:
        w_gate, w_up, w_down, layer = next_weights
        cast_tile = 2 * FF_GRANULE
        n_side = k // cast_tile
        assert k % cast_tile == 0 and m // tm >= n_side

        def row_of(i):
            return jnp.minimum(i, n_side - 1)

        def col_of(i, j):
            return jnp.where(i < n_side, jnp.minimum(j, last), last)

        in_specs += [
            pl.BlockSpec((None, cast_tile, FF_GRANULE), lambda i, j: (layer, row_of(i), col_of(i, j))),
            pl.BlockSpec((None, cast_tile, FF_GRANULE), lambda i, j: (layer, row_of(i), col_of(i, j))),
            pl.BlockSpec((None, FF_GRANULE, cast_tile), lambda i, j: (layer, col_of(i, j), row_of(i)))]
        out_specs += [
            pl.BlockSpec((cast_tile, 2 * FF_GRANULE), lambda i, j: (row_of(i), col_of(i, j))),
            pl.BlockSpec((FF_GRANULE, cast_tile),
                         lambda i, j: (jnp.where(i < n_side, j, n_blocks - 1), row_of(i)))]
        out_shape += [jax.ShapeDtypeStruct((k, 2 * d_ff), BF16),
                      jax.ShapeDtypeStruct((n_blocks * FF_GRANULE, k), BF16)]
        args += [w_gate, w_up, w_down]
    out = pl.pallas_call(
        functools.partial(_ffn_up_kernel, n_real=n_real, n_side=n_side),
        out_shape=out_shape,
        grid=(m // tm, n_blocks),
        in_specs=in_specs,
        out_specs=out_specs,
        compiler_params=_params("arbitrary", "arbitrary"),
        name=name,
    )(*args)
    return out[0] if next_weights is None else tuple(out)


def kernel(x_prompt, x_sample, mem_prompt, mem_sample, norm_mix, w_in, sg_norm, sg_w, sg_b, mla_q_norm, mla_w_uq, mla_kv_norm, mla_w_ukv, out_norm_a, out_norm_b, w_out, norm_xa, norm_mem, xa_wq, xa_wk, xa_wv, xa_wo, norm_ffn, w_gate, w_up, w_down, norm_final):
    depth, d = norm_mix.shape
    seq = x_prompt.shape[1]
    assert x_sample.shape[1] == seq and mem_sample.shape[1] == mem_prompt.shape[1]
    gw = sg_norm.shape[1]
    chunk = sg_w.shape[2]
    q_lora, kv_lora = mla_q_norm.shape[1], mla_kv_norm.shape[1]
    heads = out_norm_b.shape[1] // HEAD_DIM
    n_mem = mem_prompt.shape[1]
    assert w_in.shape[2] == 2 * gw + q_lora + kv_lora + QK_ROPE
    assert mla_w_uq.shape[2] == heads * (HEAD_DIM + QK_ROPE) and mla_w_ukv.shape[2] == heads * QK_PAD

    rows_p = x_prompt.shape[0] * seq
    rows_s = x_sample.shape[0] * seq
    x = jnp.concatenate([x_prompt.reshape(rows_p, d), x_sample.reshape(rows_s, d)], axis=0)
    mem = jnp.concatenate([mem_prompt.reshape(-1, d), mem_sample.reshape(-1, d)], axis=0)
    batch = (rows_p + rows_s) // seq

    cos_t, sin_t = _rope_tables(seq)
    q_scale = (HEAD_DIM + QK_ROPE) ** -0.5 * math.log2(math.e)

    w_gu = _layout_gate_up(w_gate, w_up, layer=0)
    w_dn = _layout_down(w_down, layer=0)
    w_out_bf = w_out.astype(BF16)
    w_uq_bf = _layout_w_uq(mla_w_uq, heads)
    w_ukv_bf = mla_w_ukv.astype(BF16)
    sg_w_bf = sg_w.astype(BF16)
    b_full = jnp.repeat(jnp.swapaxes(sg_b, 1, 2), HEAD_DIM, axis=2)
    w_xkv = jnp.concatenate([xa_wk, xa_wv], axis=2).astype(BF16)
    xa_wq_bf = xa_wq.astype(BF16)
    xa_wo_bf = xa_wo.astype(BF16)
    xw = xa_wq.shape[2]

    for l in range(depth):
        w_gated, w_latent = _layout_w_in(w_in[l], gw, q_lora, kv_lora)
        h_mix, zc = _norm_latent(x, norm_mix[l], w_latent, name=f"w_in_latent{l}")
        zg = _gelu_matmul(h_mix, w_gated, name=f"w_in_gated{l}")

        ya = _spatial_gate(zg, sg_norm[l], sg_w_bf, b_full, out_norm_a[l], layer=l, name=f"gate{l}")

        q, k, v = _mla_proj(zc, mla_q_norm[l], mla_kv_norm[l], w_uq_bf, w_ukv_bf, cos_t, sin_t,
                            layer=l, seq=seq, q_scale=q_scale, name=f"mla_proj{l}")
        yb = _mla_attention(q, k, v, out_norm_b[l], seq=seq, name=f"mla_attn{l}")

        x = _mix_out(ya, yb, w_out_bf, x, layer=l, name=f"w_out{l}")

        k_mem, v_mem = _xa_kv(mem, norm_mem[l], w_xkv, layer=l, name=f"xa_kv{l}")
        x, h_ffn = _cross_attention(x, norm_xa[l], xa_wq_bf,
                                    k_mem.reshape(batch, n_mem, xw), v_mem.reshape(batch, n_mem, xw),
                                    xa_wo_bf, norm_ffn[l], layer=l, seq=seq, name=f"xattn{l}")

        if l + 1 < depth:
            a, w_gu_next, w_dn_next = _ffn_up(h_ffn, w_gu, next_weights=(w_gate, w_up, w_down, l + 1),
                                              name=f"ffn_up{l}")
        else:
            a = _ffn_up(h_ffn, w_gu, next_weights=None, name=f"ffn_up{l}")
        x = _down_proj(a, w_dn, x, name=f"ffn_down{l}")
        if l + 1 < depth:
            w_gu, w_dn = w_gu_next, w_dn_next

    y_prompt = _rmsnorm(x, norm_final, F32, row_start=0, rows=rows_p, name="norm_final_prompt")
    y_sample = _rmsnorm(x, norm_final, F32, row_start=rows_p, rows=rows_s, name="norm_final_sample")
    return (y_prompt.reshape(x_prompt.shape), y_sample.reshape(x_sample.shape))
```

```python
import functools
import math

import jax
import jax.numpy as jnp
from jax import lax
from jax.experimental import pallas as pl
from jax.experimental.pallas import tpu as pltpu

F32 = jnp.float32
BF16 = jnp.bfloat16

EPS = 1e-6
ROPE_THETA = 10000.0
HEAD_DIM = 128
QK_ROPE = 64
ROPE_HALF = QK_ROPE // 2
QK_PAD = 2 * HEAD_DIM
LANES = 128
FF_GRANULE = 256
FF_BLOCK = 512
FF_K_SPLIT = 4
V7X_VMEM_LIMIT_BYTES = 60000 * 1024

_NT_DIMS = (((1,), (1,)), ((), ()))


def _params(*semantics):
    return pltpu.CompilerParams(dimension_semantics=semantics,
                                vmem_limit_bytes=V7X_VMEM_LIMIT_BYTES)


def _tile(n, pref):
    t = min(n, pref)
    while n % t:
        t //= 2
    return t


def _rms(x, g):
    r = lax.rsqrt(jnp.mean(x * x, axis=-1, keepdims=True) + EPS)
    return (x * r) * g


def _norm_kernel(x_ref, g_ref, o_ref):
    o_ref[...] = _rms(x_ref[...], g_ref[...]).astype(o_ref.dtype)


def _rmsnorm(x, g, out_dtype, *, row_start=0, rows=None, name):
    d = x.shape[1]
    rows = x.shape[0] if rows is None else rows
    tm = _tile(math.gcd(rows, row_start) if row_start else rows, 512)
    off = row_start // tm
    return pl.pallas_call(
        _norm_kernel,
        out_shape=jax.ShapeDtypeStruct((rows, d), out_dtype),
        grid=(rows // tm,),
        in_specs=[pl.BlockSpec((tm, d), lambda i: (i + off, 0)),
                  pl.BlockSpec((1, d), lambda i: (0, 0))],
        out_specs=pl.BlockSpec((tm, d), lambda i: (i, 0)),
        compiler_params=_params("parallel"),
        name=name,
    )(x, g.reshape(1, d))


def _norm_latent_kernel(x_ref, g_ref, w_ref, h_ref, z_ref, *, parts):
    step = x_ref.shape[0] // parts
    for part in range(parts):
        rows = slice(part * step, (part + 1) * step)
        h = _rms(x_ref[rows, :], g_ref[...]).astype(BF16)
        h_ref[rows, :] = h
        z_ref[rows, :] = jnp.dot(h, w_ref[...], preferred_element_type=F32)


def _norm_latent(x, g, w, *, name):
    t, d = x.shape
    n = w.shape[1]
    tm = _tile(t, 512)
    parts = 2 if tm % (2 * LANES) == 0 else 1
    return pl.pallas_call(
        functools.partial(_norm_latent_kernel, parts=parts),
        out_shape=(jax.ShapeDtypeStruct((t, d), BF16), jax.ShapeDtypeStruct((t, n), F32)),
        grid=(t // tm,),
        in_specs=[pl.BlockSpec((tm, d), lambda i: (i, 0)),
                  pl.BlockSpec((1, d), lambda i: (0, 0)),
                  pl.BlockSpec((d, n), lambda i: (0, 0), pipeline_mode=pl.Buffered(1))],
        out_specs=(pl.BlockSpec((tm, d), lambda i: (i, 0)), pl.BlockSpec((tm, n), lambda i: (i, 0))),
        compiler_params=_params("parallel"),
        name=name,
    )(x, g.reshape(1, d), w)


def _gelu_tanh(x):
    k1 = -2.0 * math.sqrt(2.0 / math.pi) * math.log2(math.e)
    k2 = 0.044715 * k1
    return x / (1.0 + jnp.exp2(x * (k1 + k2 * (x * x))))


def _gelu_mm_kernel(x_ref, w_ref, o_ref):
    acc = jnp.dot(x_ref[...], w_ref[...], preferred_element_type=F32)
    o_ref[...] = _gelu_tanh(acc).astype(o_ref.dtype)


def _gelu_matmul(x, w, *, name):
    m, k = x.shape
    n = w.shape[1]
    tm, tn = _tile(m, 1024), _tile(n, 1024)
    return pl.pallas_call(
        _gelu_mm_kernel,
        out_shape=jax.ShapeDtypeStruct((m, n), BF16),
        grid=(m // tm, n // tn),
        in_specs=[pl.BlockSpec((tm, k), lambda i, j: (i, 0)),
                  pl.BlockSpec((k, tn), lambda i, j: (0, j))],
        out_specs=pl.BlockSpec((tm, tn), lambda i, j: (i, j)),
        compiler_params=_params("parallel", "arbitrary"),
        name=name,
    )(x, w)


def _mix_out_kernel(ya_ref, yb_ref, wa_ref, wb_ref, x_ref, o_ref):
    acc = jnp.dot(ya_ref[...], wa_ref[...], preferred_element_type=F32)
    acc += jnp.dot(yb_ref[...], wb_ref[...], preferred_element_type=F32)
    o_ref[...] = x_ref[...] + acc


def _mix_out(ya, yb, w, x, *, layer, name):
    m, ka = ya.shape
    kb = yb.shape[1]
    assert ka == kb
    n = w.shape[2]
    tm, tn = _tile(m, 1024), _tile(n, 1024)
    return pl.pallas_call(
        _mix_out_kernel,
        out_shape=jax.ShapeDtypeStruct((m, n), F32),
        grid=(m // tm, n // tn),
        in_specs=[pl.BlockSpec((tm, ka), lambda i, j: (i, 0)),
                  pl.BlockSpec((tm, kb), lambda i, j: (i, 0)),
                  pl.BlockSpec((None, ka, tn), lambda i, j: (layer, 0, j)),
                  pl.BlockSpec((None, kb, tn), lambda i, j: (layer, 1, j)),
                  pl.BlockSpec((tm, tn), lambda i, j: (i, j))],
        out_specs=pl.BlockSpec((tm, tn), lambda i, j: (i, j)),
        compiler_params=_params("parallel", "arbitrary"),
        name=name,
    )(ya, yb, w, w, x)


def _down_kernel(a_ref, w_ref, x_ref, o_ref):
    @pl.when(pl.program_id(2) == 0)
    def _():
        o_ref[...] = x_ref[...] + jnp.dot(a_ref[...], w_ref[...], preferred_element_type=F32)

    @pl.when(pl.program_id(2) != 0)
    def _():
        o_ref[...] += jnp.dot(a_ref[...], w_ref[...], preferred_element_type=F32)


def _down_proj(a, w, x, *, name):
    m, k = a.shape
    n = w.shape[1]
    tm, tn = _tile(m, 1024), _tile(n, 1024)
    assert k % (FF_K_SPLIT * 2 * LANES) == 0
    nk = FF_K_SPLIT if k > 4096 else 1
    tk = k // nk
    return pl.pallas_call(
        _down_kernel,
        out_shape=jax.ShapeDtypeStruct((m, n), F32),
        grid=(m // tm, n // tn, nk),
        in_specs=[pl.BlockSpec((tm, tk), lambda i, j, kk: (i, kk)),
                  pl.BlockSpec((tk, tn), lambda i, j, kk: (kk, j)),
                  pl.BlockSpec((tm, tn), lambda i, j, kk: (i, j))],
        out_specs=pl.BlockSpec((tm, tn), lambda i, j, kk: (i, j)),
        compiler_params=_params("parallel", "arbitrary", "arbitrary"),
        name=name,
    )(a, w, x)


def _gate_kernel(u_ref, v_ref, gv_ref, ws_ref, b_ref, ga_ref, o_ref, mix_ref, *, n_chunks, chunk):
    groups = ws_ref.shape[0]
    vn = _rms(v_ref[...].astype(F32), gv_ref[...]).astype(BF16)
    for c in range(n_chunks):
        rows = slice(c * chunk, (c + 1) * chunk)
        for g in range(groups):
            cols = slice(g * HEAD_DIM, (g + 1) * HEAD_DIM)
            mix_ref[rows, cols] = jnp.dot(ws_ref[g], vn[rows, cols], preferred_element_type=F32)
        mix_ref[rows, :] += b_ref[...]
    ya = u_ref[...].astype(F32) * mix_ref[...]
    o_ref[...] = _rms(ya, ga_ref[...]).astype(o_ref.dtype)


def _spatial_gate(zg, g_v, w_s, b_full, g_a, *, layer, name):
    t, two_gw = zg.shape
    gw = two_gw // 2
    chunk = w_s.shape[2]
    tm = _tile(t, 2 * chunk)
    return pl.pallas_call(
        functools.partial(_gate_kernel, n_chunks=tm // chunk, chunk=chunk),
        out_shape=jax.ShapeDtypeStruct((t, gw), BF16),
        grid=(t // tm,),
        in_specs=[pl.BlockSpec((tm, gw), lambda i: (i, 0)),
                  pl.BlockSpec((tm, gw), lambda i: (i, 1)),
                  pl.BlockSpec((1, gw), lambda i: (0, 0)),
                  pl.BlockSpec((None,) + w_s.shape[1:], lambda i: (layer, 0, 0, 0)),
                  pl.BlockSpec((None, chunk, gw), lambda i: (layer, 0, 0)),
                  pl.BlockSpec((1, gw), lambda i: (0, 0))],
        out_specs=pl.BlockSpec((tm, gw), lambda i: (i, 0)),
        scratch_shapes=[pltpu.VMEM((tm, gw), F32)],
        compiler_params=_params("parallel"),
        name=name,
    )(zg, zg, g_v.reshape(1, gw), w_s, b_full, g_a.reshape(1, gw))


def _rope(slab, cos_t, sin_t):
    return slab * cos_t + pltpu.roll(slab, QK_ROPE, 1) * sin_t


def _mla_proj_kernel(zc_ref, gq_ref, gkv_ref, wq_ref, wkv_ref, cos_ref, sin_ref,
                     q_ref, k_ref, v_ref, hq_s, hkv_s, kpe_s, *, kv_lora, q_scale):
    heads = q_ref.shape[0]
    cos_t, sin_t = cos_ref[...], sin_ref[...]

    @pl.when(pl.program_id(1) == 0)
    def _():
        zc = zc_ref[...]
        hkv_s[...] = _rms(zc[:, :kv_lora], gkv_ref[...]).astype(BF16)
        kpe_s[...] = _rope(zc[:, kv_lora:kv_lora + LANES], cos_t, sin_t).astype(BF16)
        hq_s[...] = _rms(zc[:, kv_lora + LANES:], gq_ref[...]).astype(BF16)

    q_raw = jnp.dot(hq_s[...], wq_ref[...], preferred_element_type=F32)
    kv_raw = jnp.dot(hkv_s[...], wkv_ref[...], preferred_element_type=F32)
    lane = lax.broadcasted_iota(jnp.int32, (q_raw.shape[0], LANES), 1)
    ones_col = jnp.where(lane == 0, 1.0, 0.0).astype(BF16)
    for h in range(heads):
        base = h * QK_PAD
        q_ref[h, :, :HEAD_DIM] = (q_raw[:, base:base + HEAD_DIM] * q_scale).astype(BF16)
        q_pe = _rope(q_raw[:, base + HEAD_DIM:base + QK_PAD], cos_t, sin_t)
        q_ref[h, :, HEAD_DIM:] = (q_pe * q_scale).astype(BF16)
        k_ref[h, :, :HEAD_DIM] = kv_raw[:, base:base + HEAD_DIM].astype(BF16)
        k_ref[h, :, HEAD_DIM:] = kpe_s[...]
        v_ref[h, :, :HEAD_DIM] = kv_raw[:, base + HEAD_DIM:base + QK_PAD].astype(BF16)
        v_ref[h, :, HEAD_DIM:] = ones_col


def _mla_proj(zc, g_q, g_kv, w_q, w_kv, cos_t, sin_t, *, layer, seq, q_scale, name):
    t, zw = zc.shape
    kv_lora = g_kv.shape[0]
    q_lora = g_q.shape[0]
    heads = w_q.shape[2] // QK_PAD
    hb = _tile(heads, 4)
    tm = _tile(seq, 1024)
    n_pos = seq // tm
    return pl.pallas_call(
        functools.partial(_mla_proj_kernel, kv_lora=kv_lora, q_scale=q_scale),
        out_shape=(jax.ShapeDtypeStruct((heads, t, QK_PAD), BF16),
                   jax.ShapeDtypeStruct((heads, t, QK_PAD), BF16),
                   jax.ShapeDtypeStruct((heads, t, QK_PAD), BF16)),
        grid=(t // tm, heads // hb),
        in_specs=[pl.BlockSpec((tm, zw), lambda i, h: (i, 0)),
                  pl.BlockSpec((1, q_lora), lambda i, h: (0, 0)),
                  pl.BlockSpec((1, kv_lora), lambda i, h: (0, 0)),
                  pl.BlockSpec((None, q_lora, hb * QK_PAD), lambda i, h: (layer, 0, h)),
                  pl.BlockSpec((None, kv_lora, hb * QK_PAD), lambda i, h: (layer, 0, h)),
                  pl.BlockSpec((tm, LANES), lambda i, h: (i % n_pos, 0)),
                  pl.BlockSpec((tm, LANES), lambda i, h: (i % n_pos, 0))],
        out_specs=(pl.BlockSpec((hb, tm, QK_PAD), lambda i, h: (h, i, 0)),
                   pl.BlockSpec((hb, tm, QK_PAD), lambda i, h: (h, i, 0)),
                   pl.BlockSpec((hb, tm, QK_PAD), lambda i, h: (h, i, 0))),
        scratch_shapes=[pltpu.VMEM((tm, q_lora), BF16),
                        pltpu.VMEM((tm, kv_lora), BF16),
                        pltpu.VMEM((tm, LANES), BF16)],
        compiler_params=_params("parallel", "arbitrary"),
        name=name,
    )(zc, g_q.reshape(1, q_lora), g_kv.reshape(1, kv_lora), w_q, w_kv, cos_t, sin_t)


def _attn_kernel(q_ref, k_ref, v_ref, g_ref, o_ref, acc_s, *, tk):
    heads = acc_s.shape[0]
    h = pl.program_id(2)
    q = q_ref[0]
    tq = q.shape[0]
    m = jnp.full((tq, 1), -jnp.inf, F32)
    acc = jnp.zeros((tq, QK_PAD), F32)
    for c in range(k_ref.shape[1] // tk):
        rows = slice(c * tk, (c + 1) * tk)
        s = lax.dot_general(q, k_ref[0, rows, :], _NT_DIMS, preferred_element_type=F32)
        m_new = jnp.maximum(m, jnp.max(s, axis=-1, keepdims=True))
        p = jnp.exp2(s - m_new).astype(BF16)
        pv = jnp.dot(p, v_ref[0, rows, :], preferred_element_type=F32)
        acc = jnp.exp2(m - m_new) * acc + pv
        m = m_new
    acc_s[h] = acc[:, :HEAD_DIM] * (1.0 / acc[:, HEAD_DIM:HEAD_DIM + 1])

    @pl.when(h == heads - 1)
    def _():
        ssq = jnp.zeros((tq, 1), F32)
        for hh in range(heads):
            o = acc_s[hh]
            ssq += jnp.sum(o * o, axis=-1, keepdims=True)
        r = lax.rsqrt(ssq * (1.0 / (heads * HEAD_DIM)) + EPS)
        for hh in range(heads):
            cols = slice(hh * HEAD_DIM, (hh + 1) * HEAD_DIM)
            o_ref[:, cols] = ((acc_s[hh] * r) * g_ref[:, cols]).astype(o_ref.dtype)


def _mla_attention(q, k, v, g_out, *, seq, name):
    heads, t, _ = q.shape
    batch = t // seq
    tq = _tile(seq, 1024)
    tk = _tile(seq, 2 * LANES)
    nq = seq // tq
    width = heads * HEAD_DIM
    return pl.pallas_call(
        functools.partial(_attn_kernel, tk=tk),
        out_shape=jax.ShapeDtypeStruct((t, width), BF16),
        grid=(batch, nq, heads),
        in_specs=[pl.BlockSpec((1, tq, QK_PAD), lambda b, i, h: (h, b * nq + i, 0)),
                  pl.BlockSpec((1, seq, QK_PAD), lambda b, i, h: (h, b, 0)),
                  pl.BlockSpec((1, seq, QK_PAD), lambda b, i, h: (h, b, 0)),
                  pl.BlockSpec((1, width), lambda b, i, h: (0, 0))],
        out_specs=pl.BlockSpec((tq, width), lambda b, i, h: (b * nq + i, 0)),
        scratch_shapes=[pltpu.VMEM((heads, tq, HEAD_DIM), F32)],
        compiler_params=_params("parallel", "parallel", "arbitrary"),
        name=name,
    )(q, k, v, g_out.reshape(1, width))


def _xa_kv_kernel(mem_ref, g_ref, w_ref, k_ref, v_ref):
    m = _rms(mem_ref[...], g_ref[...]).astype(BF16)
    kv = jnp.dot(m, w_ref[...], preferred_element_type=F32)
    half = kv.shape[1] // 2
    k_ref[...] = kv[:, :half].astype(BF16)
    v_ref[...] = kv[:, half:].astype(BF16)


def _xa_kv(mem, g_mem, w_kv, *, layer, name):
    rows, d = mem.shape
    xw = w_kv.shape[2] // 2
    tm = _tile(rows, 256)
    return pl.pallas_call(
        _xa_kv_kernel,
        out_shape=(jax.ShapeDtypeStruct((rows, xw), BF16), jax.ShapeDtypeStruct((rows, xw), BF16)),
        grid=(rows // tm,),
        in_specs=[pl.BlockSpec((tm, d), lambda i: (i, 0)),
                  pl.BlockSpec((1, d), lambda i: (0, 0)),
                  pl.BlockSpec((None, d, 2 * xw), lambda i: (layer, 0, 0))],
        out_specs=(pl.BlockSpec((tm, xw), lambda i: (i, 0)), pl.BlockSpec((tm, xw), lambda i: (i, 0))),
        compiler_params=_params("parallel"),
        name=name,
    )(mem, g_mem.reshape(1, d), w_kv)


def _xa_kernel(x_ref, gx_ref, wq_ref, k_ref, v_ref, wo_ref, gf_ref, x2_ref, h_ref, o_s, *, scale, parts):
    k = k_ref[0]
    v = v_ref[0]
    step = x_ref.shape[0] // parts
    for part in range(parts):
        rows = slice(part * step, (part + 1) * step)
        x = x_ref[rows, :]
        hx = _rms(x, gx_ref[...]).astype(BF16)
        q = (jnp.dot(hx, wq_ref[...], preferred_element_type=F32) * scale).astype(BF16)
        for hh in range(q.shape[1] // HEAD_DIM):
            cols = slice(hh * HEAD_DIM, (hh + 1) * HEAD_DIM)
            s = lax.dot_general(q[:, cols], k[:, cols], _NT_DIMS, preferred_element_type=F32)
            e = jnp.exp(s - jnp.max(s, axis=-1, keepdims=True))
            p = e * (1.0 / jnp.sum(e, axis=-1, keepdims=True))
            o_s[rows, cols] = jnp.dot(p.astype(BF16), v[:, cols], preferred_element_type=F32).astype(BF16)
        x2 = x + jnp.dot(o_s[rows, :], wo_ref[...], preferred_element_type=F32)
        x2_ref[rows, :] = x2
        h_ref[rows, :] = _rms(x2, gf_ref[...]).astype(h_ref.dtype)


def _cross_attention(x, g_xa, w_q, k_mem, v_mem, w_o, g_ffn, *, layer, seq, name):
    t, d = x.shape
    _, n_mem, xw = k_mem.shape
    tm = _tile(seq, 512)
    parts = 2 if tm % (2 * LANES) == 0 else 1
    per_seq = seq // tm
    once = pl.Buffered(1)
    return pl.pallas_call(
        functools.partial(_xa_kernel, scale=HEAD_DIM ** -0.5, parts=parts),
        out_shape=(jax.ShapeDtypeStruct((t, d), F32), jax.ShapeDtypeStruct((t, d), BF16)),
        grid=(t // tm,),
        in_specs=[pl.BlockSpec((tm, d), lambda i: (i, 0)),
                  pl.BlockSpec((1, d), lambda i: (0, 0)),
                  pl.BlockSpec((None, d, xw), lambda i: (layer, 0, 0), pipeline_mode=once),
                  pl.BlockSpec((1, n_mem, xw), lambda i: (i // per_seq, 0, 0)),
                  pl.BlockSpec((1, n_mem, xw), lambda i: (i // per_seq, 0, 0)),
                  pl.BlockSpec((None, xw, d), lambda i: (layer, 0, 0), pipeline_mode=once),
                  pl.BlockSpec((1, d), lambda i: (0, 0))],
        out_specs=(pl.BlockSpec((tm, d), lambda i: (i, 0)), pl.BlockSpec((tm, d), lambda i: (i, 0))),
        scratch_shapes=[pltpu.VMEM((tm, xw), BF16)],
        compiler_params=_params("parallel"),
        name=name,
    )(x, g_xa.reshape(1, d), w_q, k_mem, v_mem, w_o, g_ffn.reshape(1, d))


def _rope_tables(seq):
    inv = 1.0 / (ROPE_THETA ** (jnp.arange(0, QK_ROPE, 2, dtype=F32) / QK_ROPE))
    ang = jnp.arange(seq, dtype=F32)[:, None] * inv[None, :]
    cos, sin = jnp.cos(ang), jnp.sin(ang)
    zeros = jnp.zeros((seq, LANES - QK_ROPE), F32)
    return (jnp.concatenate([cos, cos, zeros], axis=1),
            jnp.concatenate([-sin, sin, zeros], axis=1))


def _swap_halves(w):
    return jnp.concatenate([w[..., ROPE_HALF:], w[..., :ROPE_HALF]], axis=-1)


def _layout_w_in(w_in, gw, q_lora, kv_lora):
    w_c_q = w_in[..., 2 * gw:2 * gw + q_lora]
    w_c_kv = w_in[..., 2 * gw + q_lora:2 * gw + q_lora + kv_lora]
    w_kr = w_in[..., 2 * gw + q_lora + kv_lora:]
    latent = jnp.concatenate([w_c_kv, w_kr, _swap_halves(w_kr), w_c_q], axis=-1)
    return w_in[..., :2 * gw].astype(BF16), latent.astype(BF16)


def _layout_w_uq(w_uq, heads):
    lead = w_uq.shape[:-1]
    w = w_uq.reshape(lead + (heads, HEAD_DIM + QK_ROPE))
    rope = w[..., HEAD_DIM:]
    w = jnp.concatenate([w[..., :HEAD_DIM], rope, _swap_halves(rope)], axis=-1)
    return w.reshape(lead + (heads * QK_PAD,)).astype(BF16)


def _ff_pad(d_ff):
    unit = FF_K_SPLIT * 2 * LANES
    return -(-d_ff // unit) * unit


def _cast_gate_up_kernel(g_ref, u_ref, o_ref):
    for p in range(g_ref.shape[1] // FF_GRANULE):
        src = slice(p * FF_GRANULE, (p + 1) * FF_GRANULE)
        o_ref[:, 2 * p * FF_GRANULE:(2 * p + 1) * FF_GRANULE] = g_ref[:, src].astype(BF16)
        o_ref[:, (2 * p + 1) * FF_GRANULE:(2 * p + 2) * FF_GRANULE] = u_ref[:, src].astype(BF16)


def _layout_gate_up(w_gate, w_up, *, layer):
    _, d, d_ff = w_gate.shape
    assert d_ff % FF_GRANULE == 0
    tr = _tile(d, LANES)
    spec = pl.BlockSpec((None, tr, d_ff), lambda r: (layer, r, 0))
    return pl.pallas_call(
        _cast_gate_up_kernel,
        out_shape=jax.ShapeDtypeStruct((d, 2 * d_ff), BF16),
        grid=(d // tr,),
        in_specs=[spec, spec],
        out_specs=pl.BlockSpec((tr, 2 * d_ff), lambda r: (r, 0)),
        compiler_params=_params("parallel"),
        name="cast_gate_up",
    )(w_gate, w_up)


def _cast_down_kernel(w_ref, o_ref, *, d_ff):
    row = lax.broadcasted_iota(jnp.int32, w_ref.shape, 0) + pl.program_id(0) * FF_BLOCK
    o_ref[...] = jnp.where(row < d_ff, w_ref[...], 0.0).astype(BF16)


def _layout_down(w_down, *, layer):
    _, d_ff, d = w_down.shape
    n_blocks = _ff_pad(d_ff) // FF_BLOCK
    assert n_blocks * FF_BLOCK - d_ff < FF_BLOCK
    tc = _tile(d, 2048)
    return pl.pallas_call(
        functools.partial(_cast_down_kernel, d_ff=d_ff),
        out_shape=jax.ShapeDtypeStruct((n_blocks * FF_BLOCK, d), BF16),
        grid=(n_blocks, d // tc),
        in_specs=[pl.BlockSpec((None, FF_BLOCK, tc), lambda r, c: (layer, r, c))],
        out_specs=pl.BlockSpec((FF_BLOCK, tc), lambda r, c: (r, c)),
        compiler_params=_params("parallel", "parallel"),
        name="cast_down",
    )(w_down)


def _ffn_up_kernel(x_ref, w_ref, *rest, n_real, n_side):
    cast_next = n_side is not None
    if cast_next:
        g_ref, u_ref, d_ref, o_ref, gu_ref, dn_ref = rest
        casting = pl.program_id(0) < n_side
    else:
        (o_ref,) = rest

    @pl.when(pl.program_id(1) < n_real)
    def _():
        half = x_ref.shape[0] // 2
        for rows in (slice(0, half), slice(half, 2 * half)):
            acc = jnp.dot(x_ref[rows, :], w_ref[...], preferred_element_type=F32)
            o_ref[rows, :] = (jax.nn.silu(acc[:, :FF_GRANULE]) * acc[:, FF_GRANULE:]).astype(o_ref.dtype)
        if cast_next:
            @pl.when(casting)
            def _():
                gu_ref[:, :FF_GRANULE] = g_ref[...].astype(BF16)
                gu_ref[:, FF_GRANULE:] = u_ref[...].astype(BF16)
                dn_ref[...] = d_ref[...].astype(BF16)

    @pl.when(pl.program_id(1) >= n_real)
    def _():
        o_ref[...] = jnp.zeros_like(o_ref)
        if cast_next:
            @pl.when(casting)
            def _():
                dn_ref[...] = jnp.zeros_like(dn_ref)


def _ffn_up(h, w_gu, *, next_weights, name):
    m, k = h.shape
    d_ff = w_gu.shape[1] // 2
    n_real = d_ff // FF_GRANULE
    n_blocks = _ff_pad(d_ff) // FF_GRANULE
    tm = _tile(m, 2048)
    last = n_real - 1
    n_side = None
    in_specs = [pl.BlockSpec((tm, k), lambda i, j: (i, 0)),
                pl.BlockSpec((k, 2 * FF_GRANULE), lambda i, j: (0, jnp.minimum(j, last)))]
    out_specs = [pl.BlockSpec((tm, FF_GRANULE), lambda i, j: (i, j))]
    out_shape = [jax.ShapeDtypeStruct((m, n_blocks * FF_GRANULE), BF16)]
    args = [h, w_gu]
    if next_weights is not None:
        w_gate, w_up, w_down, layer = next_weights
        cast_tile = 2 * FF_GRANULE
        n_side = k // cast_tile
        assert k % cast_tile == 0 and m // tm >= n_side

        def row_of(i):
            return jnp.minimum(i, n_side - 1)

        def col_of(i, j):
            return jnp.where(i < n_side, jnp.minimum(j, last), last)

        in_specs += [
            pl.BlockSpec((None, cast_tile, FF_GRANULE), lambda i, j: (layer, row_of(i), col_of(i, j))),
            pl.BlockSpec((None, cast_tile, FF_GRANULE), lambda i, j: (layer, row_of(i), col_of(i, j))),
            pl.BlockSpec((None, FF_GRANULE, cast_tile), lambda i, j: (layer, col_of(i, j), row_of(i)))]
        out_specs += [
            pl.BlockSpec((cast_tile, 2 * FF_GRANULE), lambda i, j: (row_of(i), col_of(i, j))),
            pl.BlockSpec((FF_GRANULE, cast_tile),
                         lambda i, j: (jnp.where(i < n_side, j, n_blocks - 1), row_of(i)))]
        out_shape += [jax.ShapeDtypeStruct((k, 2 * d_ff), BF16),
                      jax.ShapeDtypeStruct((n_blocks * FF_GRANULE, k), BF16)]
        args += [w_gate, w_up, w_down]
    out = pl.pallas_call(
        functools.partial(_ffn_up_kernel, n_real=n_real, n_side=n_side),
        out_shape=out_shape,
        grid=(m // tm, n_blocks),
        in_specs=in_specs,
        out_specs=out_specs,
        compiler_params=_params("arbitrary", "arbitrary"),
        name=name,
    )(*args)
    return out[0] if next_weights is None else tuple(out)


def kernel(x_prompt, x_sample, mem_prompt, mem_sample, norm_mix, w_in, sg_norm, sg_w, sg_b, mla_q_norm, mla_w_uq, mla_kv_norm, mla_w_ukv, out_norm_a, out_norm_b, w_out, norm_xa, norm_mem, xa_wq, xa_wk, xa_wv, xa_wo, norm_ffn, w_gate, w_up, w_down, norm_final):
    depth, d = norm_mix.shape
    seq = x_prompt.shape[1]
    assert x_sample.shape[1] == seq and mem_sample.shape[1] == mem_prompt.shape[1]
    gw = sg_norm.shape[1]
    q_lora, kv_lora = mla_q_norm.shape[1], mla_kv_norm.shape[1]
    heads = out_norm_b.shape[1] // HEAD_DIM
    n_mem = mem_prompt.shape[1]
    assert w_in.shape[2] == 2 * gw + q_lora + kv_lora + QK_ROPE
    assert mla_w_uq.shape[2] == heads * (HEAD_DIM + QK_ROPE) and mla_w_ukv.shape[2] == heads * QK_PAD

    rows_p = x_prompt.shape[0] * seq
    rows_s = x_sample.shape[0] * seq
    x = jnp.concatenate([x_prompt.reshape(rows_p, d), x_sample.reshape(rows_s, d)], axis=0)
    mem = jnp.concatenate([mem_prompt.reshape(-1, d), mem_sample.reshape(-1, d)], axis=0)
    batch = (rows_p + rows_s) // seq

    cos_t, sin_t = _rope_tables(seq)
    q_scale = (HEAD_DIM + QK_ROPE) ** -0.5 * math.log2(math.e)

    w_gu = _layout_gate_up(w_gate, w_up, layer=0)
    w_dn = _layout_down(w_down, layer=0)
    w_out_bf = w_out.astype(BF16)
    w_uq_bf = _layout_w_uq(mla_w_uq, heads)
    w_ukv_bf = mla_w_ukv.astype(BF16)
    sg_w_bf = sg_w.astype(BF16)
    b_full = jnp.repeat(jnp.swapaxes(sg_b, 1, 2), HEAD_DIM, axis=2)
    w_xkv = jnp.concatenate([xa_wk, xa_wv], axis=2).astype(BF16)
    xa_wq_bf = xa_wq.astype(BF16)
    xa_wo_bf = xa_wo.astype(BF16)
    xw = xa_wq.shape[2]

    for l in range(depth):
        w_gated, w_latent = _layout_w_in(w_in[l], gw, q_lora, kv_lora)
        h_mix, zc = _norm_latent(x, norm_mix[l], w_latent, name=f"w_in_latent{l}")
        zg = _gelu_matmul(h_mix, w_gated, name=f"w_in_gated{l}")

        ya = _spatial_gate(zg, sg_norm[l], sg_w_bf, b_full, out_norm_a[l], layer=l, name=f"gate{l}")

        q, k, v = _mla_proj(zc, mla_q_norm[l], mla_kv_norm[l], w_uq_bf, w_ukv_bf, cos_t, sin_t,
                            layer=l, seq=seq, q_scale=q_scale, name=f"mla_proj{l}")
        yb = _mla_attention(q, k, v, out_norm_b[l], seq=seq, name=f"mla_attn{l}")

        x = _mix_out(ya, yb, w_out_bf, x, layer=l, name=f"w_out{l}")

        k_mem, v_mem = _xa_kv(mem, norm_mem[l], w_xkv, layer=l, name=f"xa_kv{l}")
        x, h_ffn = _cross_attention(x, norm_xa[l], xa_wq_bf,
                                    k_mem.reshape(batch, n_mem, xw), v_mem.reshape(batch, n_mem, xw),
                                    xa_wo_bf, norm_ffn[l], layer=l, seq=seq, name=f"xattn{l}")

        if l + 1 < depth:
            a, w_gu_next, w_dn_next = _ffn_up(h_ffn, w_gu, next_weights=(w_gate, w_up, w_down, l + 1),
                                              name=f"ffn_up{l}")
        else:
            a = _ffn_up(h_ffn, w_gu, next_weights=None, name=f"ffn_up{l}")
        x = _down_proj(a, w_dn, x, name=f"ffn_down{l}")
        if l + 1 < depth:
            w_gu, w_dn = w_gu_next, w_dn_next

    y_prompt = _rmsnorm(x, norm_final, F32, row_start=0, rows=rows_p, name="norm_final_prompt")
    y_sample = _rmsnorm(x, norm_final, F32, row_start=rows_p, rows=rows_s, name="norm_final_sample")
    return (y_prompt.reshape(x_prompt.shape), y_sample.reshape(x_sample.shape))
```
